```python
import math
import jax
import jax.numpy as jnp
from jax import lax
import numpy as np

D_MODEL = 1024
BATCH = 8
SEQ = 2048
DEPTH = 4
DEC_BATCH = 128
DEC_SEQ = 4
PAST_LEN = 8192
PAGE_SIZE = 128

N_PAIR = DEPTH // 2
MLA_HEADS = 8
MLA_Q_RANK = 384
MLA_KV_RANK = 256
MLA_NOPE = 64
MLA_ROPE = 32
MLA_V = 64
MLA_SCALE = (MLA_NOPE + MLA_ROPE) ** -0.5
RET_HEADS = 4
RET_DK = 128
RET_DV = 128
RET_CHUNK = 128
NSA_HEADS = 16
NSA_KV_HEADS = 2
NSA_HPG = NSA_HEADS // NSA_KV_HEADS
NSA_DH = 64
NSA_SCALE = NSA_DH ** -0.5
CMP_BLOCK = 32
CMP_HIDDEN = 64
SEL_BLOCK = 64
N_SEL = 16
WINDOW = 512
NSA_QBLOCK = 64
D_FF = 2816
CONV_W = 3
N_BUCKETS = 32
MAX_DISTANCE = 128
ROPE_BASE = 10000.0
ATTN_QBLOCK = 128
EPS = 1e-6
NEG = -1e30
FORCE = 1e4
EV_IN = MLA_Q_RANK + MLA_KV_RANK + MLA_ROPE + RET_HEADS * (2 * RET_DK + 2 * RET_DV)
EV_MIX = MLA_HEADS * MLA_V + RET_HEADS * RET_DV
OD_IN = NSA_HEADS * NSA_DH + 6 * NSA_KV_HEADS * NSA_DH + 3 * NSA_HEADS
OD_MIX = NSA_HEADS * NSA_DH

kernel_name = 'hybrid_mla_retention_nsa_convffn_step'


def rmsnorm(x, g):
    xf = x.astype(jnp.float32)
    y = xf * lax.rsqrt(jnp.mean(xf * xf, axis=-1, keepdims=True) + EPS)
    return (y * g.astype(jnp.float32)).astype(x.dtype)


def rope(x, pos):
    half = x.shape[-1] // 2
    inv = ROPE_BASE ** (-jnp.arange(half, dtype=jnp.float32) / half)
    ang = pos.astype(jnp.float32)[:, None] * inv[None, :]
    ang = ang.reshape((ang.shape[0],) + (1,) * (x.ndim - 3) + (half,))
    cos, sin = jnp.cos(ang), jnp.sin(ang)
    xf = x.astype(jnp.float32)
    x1, x2 = xf[..., :half], xf[..., half:]
    return jnp.concatenate([x1 * cos - x2 * sin, x2 * cos + x1 * sin], axis=-1).astype(x.dtype)


def t5_bucket(dist):
    n = jnp.maximum(dist, 0)
    exact = N_BUCKETS // 2
    nf = jnp.maximum(n, 1).astype(jnp.float32)
    large = exact + (jnp.log(nf / exact) / math.log(MAX_DISTANCE / exact) * (N_BUCKETS - exact)).astype(jnp.int32)
    return jnp.where(n < exact, n, jnp.minimum(large, N_BUCKETS - 1))


def t5_bias(rel_bias, bucket):
    tbl = rel_bias.astype(jnp.float32).T.reshape(NSA_KV_HEADS, NSA_HPG, N_BUCKETS)
    g_ix = jnp.arange(NSA_KV_HEADS)[:, None, None, None]
    h_ix = jnp.arange(NSA_HPG)[None, :, None, None]
    return tbl[g_ix, h_ix, bucket[..., None, :, :]]


def joint_softmax(s1, s2):
    m = jnp.maximum(s1.max(-1, keepdims=True), s2.max(-1, keepdims=True))
    e1 = jnp.exp(s1 - m)
    e2 = jnp.exp(s2 - m)
    z = e1.sum(-1, keepdims=True) + e2.sum(-1, keepdims=True)
    return e1 / z, e2 / z


def even_proj(h, pos, w_in, q_norm, w_qb, kv_norm, w_uk):
    B, T, _ = h.shape
    z = h @ w_in
    a = MLA_Q_RANK
    b = a + MLA_KV_RANK
    c = b + MLA_ROPE
    nq = RET_HEADS * RET_DK
    nv = RET_HEADS * RET_DV
    q = (rmsnorm(z[..., :a], q_norm) @ w_qb).reshape(B, T, MLA_HEADS, MLA_NOPE + MLA_ROPE)
    q_abs = jnp.einsum('bthd,chd->bthc', q[..., :MLA_NOPE], w_uk) * MLA_SCALE
    q_pe = rope(q[..., MLA_NOPE:], pos) * MLA_SCALE
    ckv = rmsnorm(z[..., a:b], kv_norm)
    krope = rope(z[..., b:c], pos)
    r = z[..., c:]
    rq = rope(r[..., :nq].reshape(B, T, RET_HEADS, RET_DK), pos)
    rk = rope(r[..., nq:2 * nq].reshape(B, T, RET_HEADS, RET_DK), pos) * RET_DK ** -0.5
    rv = r[..., 2 * nq:2 * nq + nv].reshape(B, T, RET_HEADS, RET_DV)
    rg = r[..., 2 * nq + nv:]
    return q_abs, q_pe, ckv, krope, rq, rk, rv, rg


def mla_prompt_attn(q_abs, q_pe, ckv, krope):
    B, T, H, C = q_abs.shape
    kpos = jnp.arange(T)

    def block(i):
        s0 = i * ATTN_QBLOCK
        qa = lax.dynamic_slice_in_dim(q_abs, s0, ATTN_QBLOCK, axis=1)
        qp = lax.dynamic_slice_in_dim(q_pe, s0, ATTN_QBLOCK, axis=1)
        s = (jnp.einsum('bqhc,bkc->bhqk', qa, ckv) + jnp.einsum('bqhr,bkr->bhqk', qp, krope)).astype(jnp.float32)
        mask = kpos[None, :] <= (s0 + jnp.arange(ATTN_QBLOCK))[:, None]
        p = jax.nn.softmax(jnp.where(mask, s, NEG), axis=-1)
        return jnp.einsum('bhqk,bkc->bqhc', p.astype(ckv.dtype), ckv)

    o = lax.map(block, jnp.arange(T // ATTN_QBLOCK))
    return o.transpose(1, 0, 2, 3, 4).reshape(B, T, H, C)


def mla_sample_attn(q_abs, q_pe, ckv, krope, ckv_past, kr_past, pos):
    s_past = (jnp.einsum('bqhc,bkc->bhqk', q_abs, ckv_past) + jnp.einsum('bqhr,bkr->bhqk', q_pe, kr_past)).astype(jnp.float32)
    s_new = (jnp.einsum('bqhc,bkc->bhqk', q_abs, ckv) + jnp.einsum('bqhr,bkr->bhqk', q_pe, krope)).astype(jnp.float32)
    s_new = jnp.where(pos[None, :] <= pos[:, None], s_new, NEG)
    p_past, p_new = joint_softmax(s_past, s_new)
    return (jnp.einsum('bhqk,bkc->bqhc', p_past.astype(ckv.dtype), ckv_past)
            + jnp.einsum('bhqk,bkc->bqhc', p_new.astype(ckv.dtype), ckv))


def retention(q, k, v, state0, chunk):
    B, T, H, DK = q.shape
    DV = v.shape[-1]
    nc = T // chunk
    f32 = jnp.float32
    log_g = jnp.log(1.0 - 2.0 ** (-5.0 - jnp.arange(H, dtype=f32)))
    idx = jnp.arange(chunk, dtype=f32)
    diff = idx[:, None] - idx[None, :]
    dmask = jnp.where(diff >= 0, jnp.exp(jnp.maximum(diff, 0.0) * log_g[:, None, None]), 0.0)
    q_dec = jnp.exp((idx + 1.0) * log_g[:, None])
    k_dec = jnp.exp((chunk - 1.0 - idx) * log_g[:, None])
    c_dec = jnp.exp(chunk * log_g)

    def to_chunks(a):
        return a.astype(f32).reshape(B, nc, chunk, H, a.shape[-1]).transpose(1, 0, 3, 2, 4)

    def step(S, inp):
        qc, kc, vc = inp
        inner = jnp.einsum('bhqd,bhkd->bhqk', qc, kc) * dmask
        o = jnp.einsum('bhqk,bhkv->bhqv', inner, vc) + jnp.einsum('bhqd,bhdv->bhqv', qc * q_dec[..., None], S)
        S = S * c_dec[:, None, None] + jnp.einsum('bhkd,bhkv->bhdv', kc * k_dec[..., None], vc)
        return S, o

    S, o = lax.scan(step, state0.astype(f32), (to_chunks(q), to_chunks(k), to_chunks(v)))
    return o.transpose(1, 0, 3, 2, 4).reshape(B, T, H, DV), S


def even_out(o_lat, ret_o, rg, w_uv, w_out):
    B, T = o_lat.shape[:2]
    o_mla = jnp.einsum('bthc,chd->bthd', o_lat, w_uv).reshape(B, T, MLA_HEADS * MLA_V)
    rn = ret_o * lax.rsqrt(jnp.mean(ret_o * ret_o, axis=-1, keepdims=True) + EPS)
    o_ret = jax.nn.silu(rg.astype(jnp.float32)) * rn.reshape(B, T, RET_HEADS * RET_DV)
    return jnp.concatenate([o_mla, o_ret.astype(o_mla.dtype)], axis=-1) @ w_out


def odd_proj(h, w_in):
    B, T, _ = h.shape
    z = h @ w_in
    nq = NSA_HEADS * NSA_DH
    nkv = 6 * NSA_KV_HEADS * NSA_DH
    q = z[..., :nq].reshape(B, T, NSA_KV_HEADS, NSA_HPG, NSA_DH)
    kv = z[..., nq:nq + nkv].reshape(B, T, 3, 2, NSA_KV_HEADS, NSA_DH)
    gates = jax.nn.sigmoid(z[..., nq + nkv:].astype(jnp.float32)).reshape(B, T, NSA_KV_HEADS, NSA_HPG, 3)
    return q, kv, gates


def compress(rows, pe, w1, w2):
    B, S, G, DH = rows.shape
    nc = S // CMP_BLOCK
    blk = rows[:, :nc * CMP_BLOCK].reshape(B, nc, CMP_BLOCK, G, DH) + pe[None, None, :, None, :]
    flat = blk.transpose(0, 1, 3, 2, 4).reshape(B, nc, G, CMP_BLOCK * DH)
    return jax.nn.silu(flat @ w1) @ w2


def nsa_cmp_branch(q, q_pos, kc, vc, rel_bias):
    end = jnp.arange(kc.shape[1]) * CMP_BLOCK + (CMP_BLOCK - 1)
    dist = q_pos[:, None] - end[None, :]
    mask = dist >= 0
    s = jnp.einsum('bqghd,bkgd->bghqk', q, kc).astype(jnp.float32) * NSA_SCALE + t5_bias(rel_bias, t5_bucket(dist)[None])
    p = jax.nn.softmax(jnp.where(mask, s, NEG), axis=-1) * mask
    return jnp.einsum('bghqk,bkgd->bqghd', p.astype(vc.dtype), vc), p


def nsa_select(p_cmp, q_pos, n_blk):
    B, G, _, Tq, NC = p_cmp.shape
    ratio = SEL_BLOCK // CMP_BLOCK
    imp = jnp.pad(p_cmp.sum(axis=2), ((0, 0), (0, 0), (0, 0), (0, n_blk * ratio - NC)))
    imp = imp.reshape(B, G, Tq, n_blk, ratio).sum(-1)
    blk = jnp.arange(n_blk)[None, :]
    cur = (q_pos // SEL_BLOCK)[:, None]
    forced = (blk == 0) | (blk == cur) | (blk == cur - 1)
    valid = blk * SEL_BLOCK <= q_pos[:, None]
    score = jnp.where(valid, jnp.where(forced, FORCE, imp), -1.0)
    _, idx = lax.top_k(score, min(N_SEL, n_blk))
    ok = valid[jnp.arange(Tq)[:, None], idx]
    return idx, ok


def nsa_sel_attend(q, q_pos, kg, vg, kpos, kmask, rel_bias):
    s = jnp.einsum('bqghd,bgqkd->bghqk', q, kg).astype(jnp.float32) * NSA_SCALE
    s = s + t5_bias(rel_bias, t5_bucket(q_pos[:, None] - kpos))
    m = kmask[:, :, None]
    p = jax.nn.softmax(jnp.where(m, s, NEG), axis=-1) * m
    return jnp.einsum('bghqk,bgqkd->bqghd', p.astype(vg.dtype), vg)


def nsa_win_attend(q, q_pos, kw, vw, kpos, rel_bias):
    dist = q_pos[:, None] - kpos[None, :]
    mask = (dist >= 0) & (dist < WINDOW) & (kpos >= 0)[None, :]
    s = jnp.einsum('bqghd,bkgd->bghqk', q, kw).astype(jnp.float32) * NSA_SCALE + t5_bias(rel_bias, t5_bucket(dist)[None])
    p = jax.nn.softmax(jnp.where(mask, s, NEG), axis=-1) * mask
    return jnp.einsum('bghqk,bkgd->bqghd', p.astype(vw.dtype), vw)


def nsa_prompt(q, kv, pe, w1, w2, rel_bias):
    B, T = q.shape[:2]
    G, DH = NSA_KV_HEADS, NSA_DH
    kc = compress(kv[:, :, 0, 0], pe[0], w1[0], w2[0])
    vc = compress(kv[:, :, 0, 1], pe[1], w1[1], w2[1])
    n_blk = T // SEL_BLOCK
    ks_blk = kv[:, :, 1, 0].reshape(B, n_blk, SEL_BLOCK, G, DH)
    vs_blk = kv[:, :, 1, 1].reshape(B, n_blk, SEL_BLOCK, G, DH)
    kw_pad = jnp.pad(kv[:, :, 2, 0], ((0, 0), (WINDOW, 0), (0, 0), (0, 0)))
    vw_pad = jnp.pad(kv[:, :, 2, 1], ((0, 0), (WINDOW, 0), (0, 0), (0, 0)))
    b_ix = jnp.arange(B)[:, None, None, None]
    g_ix = jnp.arange(G)[None, :, None, None]
    off = jnp.arange(SEL_BLOCK)

    def block(i):
        s0 = i * NSA_QBLOCK
        qi = lax.dynamic_slice_in_dim(q, s0, NSA_QBLOCK, axis=1)
        q_pos = s0 + jnp.arange(NSA_QBLOCK)
        o_c, p_c = nsa_cmp_branch(qi, q_pos, kc, vc, rel_bias)
        idx, ok = nsa_select(p_c, q_pos, n_blk)
        kg = ks_blk[b_ix, idx, :, g_ix]
        vg = vs_blk[b_ix, idx, :, g_ix]
        kpos = idx[..., None] * SEL_BLOCK + off
        kmask = ok[..., None] & (kpos <= q_pos[:, None, None])
        nk = kpos.shape[3] * SEL_BLOCK
        o_s = nsa_sel_attend(qi, q_pos, kg.reshape(B, G, NSA_QBLOCK, nk, DH), vg.reshape(B, G, NSA_QBLOCK, nk, DH),
                             kpos.reshape(B, G, NSA_QBLOCK, nk), kmask.reshape(B, G, NSA_QBLOCK, nk), rel_bias)
        kw_i = lax.dynamic_slice_in_dim(kw_pad, s0, WINDOW + NSA_QBLOCK, axis=1)
        vw_i = lax.dynamic_slice_in_dim(vw_pad, s0, WINDOW + NSA_QBLOCK, axis=1)
        kpos_w = s0 - WINDOW + jnp.arange(WINDOW + NSA_QBLOCK)
        o_w = nsa_win_attend(qi, q_pos, kw_i, vw_i, kpos_w, rel_bias)
        return o_c, o_s, o_w

    o_c, o_s, o_w = lax.map(block, jnp.arange(T // NSA_QBLOCK))

    def unblock(o):
        return o.transpose(1, 0, 2, 3, 4, 5).reshape(B, T, G, NSA_HPG, DH)

    return unblock(o_c), unblock(o_s), unblock(o_w)


def nsa_sample(q, kv, pos, cache_cmp, cache_sel, page_table, win_buf, j, pe, w1, w2, rel_bias):
    DB, T = q.shape[:2]
    G, DH = NSA_KV_HEADS, NSA_DH
    past = page_table.shape[1] * PAGE_SIZE
    past_cmp = cache_cmp[j, page_table].reshape(DB, past, 2, G, DH)
    kc = jnp.concatenate([compress(past_cmp[:, :, 0], pe[0], w1[0], w2[0]), compress(kv[:, :, 0, 0], pe[0], w1[0], w2[0])], axis=1)
    vc = jnp.concatenate([compress(past_cmp[:, :, 1], pe[1], w1[1], w2[1]), compress(kv[:, :, 0, 1], pe[1], w1[1], w2[1])], axis=1)
    o_c, p_c = nsa_cmp_branch(q, pos, kc, vc, rel_bias)
    n_blk = -(-(past + T) // SEL_BLOCK)
    idx, ok = nsa_select(p_c, pos, n_blk)
    past_blk = past // SEL_BLOCK
    sub = PAGE_SIZE // SEL_BLOCK
    off = jnp.arange(SEL_BLOCK)
    safe = jnp.minimum(idx, past_blk - 1)
    b_ix = jnp.arange(DB)[:, None, None, None]
    page = page_table[b_ix, safe // sub]
    tok = (safe % sub)[..., None] * SEL_BLOCK + off
    g5 = jnp.arange(G)[None, :, None, None, None]
    kg = cache_sel[j, page[..., None], tok, 0, g5]
    vg = cache_sel[j, page[..., None], tok, 1, g5]
    nk = idx.shape[3] * SEL_BLOCK
    kpos_p = (idx[..., None] * SEL_BLOCK + off).reshape(DB, G, T, nk)
    kmask_p = jnp.broadcast_to((ok & (idx < past_blk))[..., None], (DB, G, T, idx.shape[3], SEL_BLOCK)).reshape(DB, G, T, nk)
    kn = jnp.broadcast_to(kv[:, :, 1, 0].transpose(0, 2, 1, 3)[:, :, None], (DB, G, T, T, DH))
    vn = jnp.broadcast_to(kv[:, :, 1, 1].transpose(0, 2, 1, 3)[:, :, None], (DB, G, T, T, DH))
    kpos_n = jnp.broadcast_to(pos, (DB, G, T, T))
    member = jnp.any((idx[..., None] == (pos // SEL_BLOCK)) & ok[..., None], axis=-2)
    kmask_n = member & (pos[None, :] <= pos[:, None])
    o_s = nsa_sel_attend(q, pos,
                         jnp.concatenate([kg.reshape(DB, G, T, nk, DH), kn], axis=3),
                         jnp.concatenate([vg.reshape(DB, G, T, nk, DH), vn], axis=3),
                         jnp.concatenate([kpos_p, kpos_n], axis=3),
                         jnp.concatenate([kmask_p, kmask_n], axis=3), rel_bias)
    wb = win_buf.shape[1]
    win_cat = jnp.concatenate([win_buf.astype(kv.dtype), kv[:, :, 2]], axis=1)
    kpos_w = past - wb + jnp.arange(wb + T)
    o_w = nsa_win_attend(q, pos, win_cat[:, :, 0], win_cat[:, :, 1], kpos_w, rel_bias)
    return o_c, o_s, o_w, win_cat[:, T:]


def odd_out(o_c, o_s, o_w, gates, w_out):
    B, T = o_c.shape[:2]
    o = gates[..., 0:1] * o_c + gates[..., 1:2] * o_s + gates[..., 2:3] * o_w
    return o.astype(w_out.dtype).reshape(B, T, OD_MIX) @ w_out


def conv_ffn(h, conv_state, w_up, conv_w, conv_b, w_down):
    T = h.shape[1]
    up = h @ w_up
    a, u = up[..., :D_FF], up[..., D_FF:]
    a_ext = jnp.concatenate([conv_state.astype(a.dtype), a], axis=1)
    c = conv_b
    for k in range(CONV_W):
        c = c + a_ext[:, k:k + T] * conv_w[k]
    y = (jax.nn.silu(c) * u) @ w_down
    return y, a_ext[:, T:]


def setup_inputs(seed: int = 0) -> dict:
    key = jax.random.key(seed)
    keys = jax.random.split(key, 32)
    f32 = jnp.float32
    n_pages = PAST_LEN // PAGE_SIZE
    pool = (5 * DEC_BATCH * n_pages) // 4
    wb = min(WINDOW, PAST_LEN)

    def nrm(i, shape, scale=None):
        r = jax.random.normal(keys[i], shape, f32)
        return r if scale is None else r * scale

    def gain(i, shape):
        return 1.0 + 0.05 * jax.random.normal(keys[i], shape, f32)

    page_table = jax.random.permutation(keys[6], pool)[:DEC_BATCH * n_pages].reshape(DEC_BATCH, n_pages).astype(jnp.int32)
    return {
        'x_prompt': nrm(0, (BATCH, SEQ, D_MODEL)),
        'x_sample': nrm(1, (DEC_BATCH, DEC_SEQ, D_MODEL)),
        'cache_mla_ckv': nrm(2, (N_PAIR, pool, PAGE_SIZE, MLA_KV_RANK)),
        'cache_mla_krope': nrm(3, (N_PAIR, pool, PAGE_SIZE, MLA_ROPE)),
        'cache_nsa_cmp': nrm(4, (N_PAIR, pool, PAGE_SIZE, 2, NSA_KV_HEADS, NSA_DH)),
        'cache_nsa_sel': nrm(5, (N_PAIR, pool, PAGE_SIZE, 2, NSA_KV_HEADS, NSA_DH)),
        'page_table': page_table,
        'state_nsa_win': nrm(7, (N_PAIR, DEC_BATCH, wb, 2, NSA_KV_HEADS, NSA_DH)),
        'state_ret': nrm(8, (N_PAIR, DEC_BATCH, RET_HEADS, RET_DK, RET_DV), 0.5),
        'state_ffn_conv': nrm(9, (DEPTH, DEC_BATCH, CONV_W - 1, D_FF)),
        'rel_bias': nrm(10, (N_BUCKETS, NSA_HEADS), 0.5),
        'norm_mix': gain(11, (DEPTH, D_MODEL)),
        'norm_ffn': gain(12, (DEPTH, D_MODEL)),
        'final_norm': gain(13, (D_MODEL,)),
        'ev_w_in': nrm(14, (N_PAIR, D_MODEL, EV_IN), D_MODEL ** -0.5),
        'mla_q_norm': gain(15, (N_PAIR, MLA_Q_RANK)),
        'mla_w_qb': nrm(16, (N_PAIR, MLA_Q_RANK, MLA_HEADS * (MLA_NOPE + MLA_ROPE)), MLA_Q_RANK ** -0.5),
        'mla_kv_norm': gain(17, (N_PAIR, MLA_KV_RANK)),
        'mla_w_uk': nrm(18, (N_PAIR, MLA_KV_RANK, MLA_HEADS, MLA_NOPE), MLA_KV_RANK ** -0.5),
        'mla_w_uv': nrm(19, (N_PAIR, MLA_KV_RANK, MLA_HEADS, MLA_V), MLA_KV_RANK ** -0.5),
        'ev_w_out': nrm(20, (N_PAIR, EV_MIX, D_MODEL), EV_MIX ** -0.5),
        'od_w_in': nrm(21, (N_PAIR, D_MODEL, OD_IN), D_MODEL ** -0.5),
        'cmp_pe': nrm(22, (N_PAIR, 2, CMP_BLOCK, NSA_DH), 0.5),
        'cmp_w1': nrm(23, (N_PAIR, 2, CMP_BLOCK * NSA_DH, CMP_HIDDEN), (CMP_BLOCK * NSA_DH) ** -0.5),
        'cmp_w2': nrm(24, (N_PAIR, 2, CMP_HIDDEN, NSA_DH), CMP_HIDDEN ** -0.5),
        'od_w_out': nrm(25, (N_PAIR, OD_MIX, D_MODEL), OD_MIX ** -0.5),
        'ffn_w_up': nrm(26, (DEPTH, D_MODEL, 2 * D_FF), D_MODEL ** -0.5),
        'ffn_conv_w': nrm(27, (DEPTH, CONV_W, D_FF), CONV_W ** -0.5),
        'ffn_conv_b': nrm(28, (DEPTH, D_FF), 0.02),
        'ffn_w_down': nrm(29, (DEPTH, D_FF, D_MODEL), D_FF ** -0.5),
    }


def reference(x_prompt, x_sample, cache_mla_ckv, cache_mla_krope, cache_nsa_cmp, cache_nsa_sel, page_table,
              state_nsa_win, state_ret, state_ffn_conv, rel_bias, norm_mix, norm_ffn, final_norm,
              ev_w_in, mla_q_norm, mla_w_qb, mla_kv_norm, mla_w_uk, mla_w_uv, ev_w_out,
              od_w_in, cmp_pe, cmp_w1, cmp_w2, od_w_out, ffn_w_up, ffn_conv_w, ffn_conv_b, ffn_w_down):
    B, T, _ = x_prompt.shape
    DB, TS, _ = x_sample.shape
    past = page_table.shape[1] * PAGE_SIZE
    pos_p = jnp.arange(T)
    pos_s = past + jnp.arange(TS)
    hp, hs = x_prompt, x_sample
    ckv_p, ckv_s, kr_p, kr_s = [], [], [], []
    cmp_p, cmp_s, sel_p, sel_s, win_p, win_s = [], [], [], [], [], []
    ret_p, ret_s, conv_p, conv_s = [], [], [], []
    for layer in range(DEPTH):
        j = layer // 2
        hn_p = rmsnorm(hp, norm_mix[layer])
        hn_s = rmsnorm(hs, norm_mix[layer])
        if layer % 2 == 0:
            w = (ev_w_in[j], mla_q_norm[j], mla_w_qb[j], mla_kv_norm[j], mla_w_uk[j])
            qa, qp, ckv, kr, rq, rk, rv, rg = even_proj(hn_p, pos_p, *w)
            o_lat = mla_prompt_attn(qa, qp, ckv, kr)
            r_o, r_S = retention(rq, rk, rv, jnp.zeros((B, RET_HEADS, RET_DK, RET_DV), jnp.float32), min(RET_CHUNK, T))
            hp = hp + even_out(o_lat, r_o, rg, mla_w_uv[j], ev_w_out[j])
            ckv_p.append(ckv)
            kr_p.append(kr)
            ret_p.append(r_S)
            qa, qp, ckv, kr, rq, rk, rv, rg = even_proj(hn_s, pos_s, *w)
            ckv_past = cache_mla_ckv[j, page_table].reshape(DB, past, MLA_KV_RANK)
            kr_past = cache_mla_krope[j, page_table].reshape(DB, past, MLA_ROPE)
            o_lat = mla_sample_attn(qa, qp, ckv, kr, ckv_past, kr_past, pos_s)
            r_o, r_S = retention(rq, rk, rv, state_ret[j], TS)
            hs = hs + even_out(o_lat, r_o, rg, mla_w_uv[j], ev_w_out[j])
            ckv_s.append(ckv)
            kr_s.append(kr)
            ret_s.append(r_S)
        else:
            q, kv, gates = odd_proj(hn_p, od_w_in[j])
            o_c, o_s, o_w = nsa_prompt(q, kv, cmp_pe[j], cmp_w1[j], cmp_w2[j], rel_bias)
            hp = hp + odd_out(o_c, o_s, o_w, gates, od_w_out[j])
            cmp_p.append(kv[:, :, 0])
            sel_p.append(kv[:, :, 1])
            win_p.append(kv[:, T - min(WINDOW, T):, 2])
            q, kv, gates = odd_proj(hn_s, od_w_in[j])
            o_c, o_s, o_w, win_new = nsa_sample(q, kv, pos_s, cache_nsa_cmp, cache_nsa_sel, page_table, state_nsa_win[j], j,
                                                cmp_pe[j], cmp_w1[j], cmp_w2[j], rel_bias)
            hs = hs + odd_out(o_c, o_s, o_w, gates, od_w_out[j])
            cmp_s.append(kv[:, :, 0])
            sel_s.append(kv[:, :, 1])
            win_s.append(win_new)
        fw = (ffn_w_up[layer], ffn_conv_w[layer], ffn_conv_b[layer], ffn_w_down[layer])
        y, c = conv_ffn(rmsnorm(hp, norm_ffn[layer]), jnp.zeros((B, CONV_W - 1, D_FF), hp.dtype), *fw)
        hp = hp + y
        conv_p.append(c)
        y, c = conv_ffn(rmsnorm(hs, norm_ffn[layer]), state_ffn_conv[layer], *fw)
        hs = hs + y
        conv_s.append(c)
    y_prompt = rmsnorm(hp, final_norm)
    y_sample = rmsnorm(hs, final_norm)
    return (y_prompt, y_sample,
            jnp.stack(ckv_p), jnp.stack(ckv_s), jnp.stack(kr_p), jnp.stack(kr_s),
            jnp.stack(cmp_p), jnp.stack(cmp_s), jnp.stack(sel_p), jnp.stack(sel_s),
            jnp.stack(win_p), jnp.stack(win_s), jnp.stack(ret_p), jnp.stack(ret_s),
            jnp.stack(conv_p), jnp.stack(conv_s))
```

```python
import functools
import math

import numpy as np
import jax
import jax.numpy as jnp
from jax import lax
from jax.experimental import pallas as pl
from jax.experimental.pallas import tpu as pltpu

f32 = jnp.float32
bf16 = jnp.bfloat16

PAGE = 128
MLA_HEADS = 8
MLA_Q_RANK = 384
MLA_KV_RANK = 256
MLA_NOPE = 64
MLA_ROPE = 32
MLA_V = 64
MLA_SCALE = (MLA_NOPE + MLA_ROPE) ** -0.5
RET_HEADS = 4
RET_D = 128
NSA_HEADS = 16
NSA_G = 2
NSA_HPG = 8
NSA_DH = 64
NSA_SCALE = NSA_DH ** -0.5
CMP_BLOCK = 32
SEL_BLOCK = 64
N_SEL = 16
WINDOW = 512
D_FF = 2816
N_BUCKETS = 32
MAX_DISTANCE = 128
ROPE_BASE = 10000.0
EPS = 1e-6
NEG = -1e30
FORCE = 1e4
KV_LANES = 2 * NSA_G * NSA_DH

VMEM_LIMIT = 56 * 1024 * 1024


def _cparams(sem):
    return pltpu.CompilerParams(dimension_semantics=sem, vmem_limit_bytes=VMEM_LIMIT)


def _dot(a, b):
    return jnp.dot(a, b, preferred_element_type=f32)


def _dot_nt(a, b):
    return lax.dot_general(a, b, (((1,), (1,)), ((), ())), preferred_element_type=f32)


def _mm_kernel(*refs, norm, resid):
    it = iter(refs)
    x_ref = next(it)
    g_ref = next(it) if norm else None
    w_ref = next(it)
    r_ref = next(it) if resid else None
    o_ref = next(it)
    xn_ref = next(it)

    @pl.when(pl.program_id(1) == 0)
    def _():
        x = x_ref[...]
        if norm:
            x = x * lax.rsqrt(jnp.mean(x * x, axis=-1, keepdims=True) + EPS) * g_ref[...]
        xn_ref[...] = x.astype(bf16)

    acc = _dot(xn_ref[...], w_ref[...])
    if resid:
        acc = acc + r_ref[...]
    o_ref[...] = acc


def _row_tile(m, cap):
    for t in (512, 256, 128, 64, 32, 16, 8):
        if t <= cap and m % t == 0:
            return t
    return m


def _pick_tile(n, cap):
    best = None
    for t in range(128, min(n, cap) + 1, 128):
        if n % t == 0:
            best = t
    return best if best is not None else n


def _mm(x, w, g=None, resid=None, tm=512):
    M = x.shape[0]
    K, N = w.shape
    tm = _row_tile(M, tm)
    assert K == x.shape[1] or K % 128 == 0
    tn = _pick_tile(N, 1536)
    norm = g is not None
    has_r = resid is not None
    in_specs = [pl.BlockSpec((tm, K), lambda i, j: (i, 0))]
    args = [x]
    if norm:
        in_specs.append(pl.BlockSpec((1, K), lambda i, j: (0, 0)))
        args.append(g.reshape(1, K).astype(f32))
    in_specs.append(pl.BlockSpec((K, tn), lambda i, j: (0, j)))
    args.append(w)
    if has_r:
        in_specs.append(pl.BlockSpec((tm, tn), lambda i, j: (i, j)))
        args.append(resid)
    return pl.pallas_call(
        functools.partial(_mm_kernel, norm=norm, resid=has_r),
        out_shape=jax.ShapeDtypeStruct((M, N), f32),
        grid=(M // tm, N // tn),
        in_specs=in_specs,
        out_specs=pl.BlockSpec((tm, tn), lambda i, j: (i, j)),
        scratch_shapes=[pltpu.VMEM((tm, K), bf16)],
        compiler_params=_cparams(("parallel", "arbitrary")),
        name="mm",
    )(*args)


def _rmsnorm_kernel(x_ref, g_ref, o_ref):
    x = x_ref[...]
    o_ref[...] = x * lax.rsqrt(jnp.mean(x * x, axis=-1, keepdims=True) + EPS) * g_ref[...]


def _rmsnorm(x, g, tm=512):
    M, D = x.shape
    tm = _row_tile(M, tm)
    return pl.pallas_call(
        _rmsnorm_kernel,
        out_shape=jax.ShapeDtypeStruct((M, D), f32),
        grid=(M // tm,),
        in_specs=[pl.BlockSpec((tm, D), lambda i: (i, 0)), pl.BlockSpec((1, D), lambda i: (0, 0))],
        out_specs=pl.BlockSpec((tm, D), lambda i: (i, 0)),
        compiler_params=_cparams(("parallel",)),
        name="rmsnorm",
    )(x, g.reshape(1, D))


def _ffn_down_kernel(a_ref, a1_ref, a2_ref, u_ref, cw_ref, cb_ref, w_ref, r_ref, o_ref):
    cw = cw_ref[...]
    c = cb_ref[...] + a2_ref[...] * cw[0:1] + a1_ref[...] * cw[1:2] + a_ref[...] * cw[2:3]
    gate = c * jax.nn.sigmoid(c) * u_ref[...]
    o_ref[...] = _dot(gate.astype(bf16), w_ref[...]) + r_ref[...]


def _ffn_down(up, a1, a2, conv_w, conv_b, w_down, resid, tm=256):
    M = up.shape[0]
    D = w_down.shape[1]
    tm = _row_tile(M, tm)
    row = lambda i: (i, 0)
    return pl.pallas_call(
        _ffn_down_kernel,
        out_shape=jax.ShapeDtypeStruct((M, D), f32),
        grid=(M // tm,),
        in_specs=[
            pl.BlockSpec((tm, D_FF), row),
            pl.BlockSpec((tm, D_FF), row),
            pl.BlockSpec((tm, D_FF), row),
            pl.BlockSpec((tm, D_FF), lambda i: (i, 1)),
            pl.BlockSpec((3, D_FF), lambda i: (0, 0)),
            pl.BlockSpec((1, D_FF), lambda i: (0, 0)),
            pl.BlockSpec((D_FF, D), lambda i: (0, 0)),
            pl.BlockSpec((tm, D), row),
        ],
        out_specs=pl.BlockSpec((tm, D), row),
        compiler_params=_cparams(("parallel",)),
        name="ffn_down",
    )(up, a1, a2, up, conv_w, conv_b.reshape(1, D_FF), w_down, resid)


def _retention_kernel(q_ref, k_ref, v_ref, dm_ref, qd_ref, kd_ref, cd_ref, s0_ref, o_ref, so_ref, s_scr):
    c = pl.program_id(2)

    @pl.when(c == 0)
    def _():
        s_scr[...] = s0_ref[0, 0]

    q = q_ref[0]
    k = k_ref[0]
    v = v_ref[0].astype(bf16)
    S = s_scr[...]
    inner = _dot_nt(q.astype(bf16), k.astype(bf16)) * dm_ref[0]
    o = _dot(inner.astype(bf16), v) + _dot((q * qd_ref[0]).astype(bf16), S.astype(bf16))
    o_ref[0] = o
    kd = (k * kd_ref[0]).T
    s_new = S * cd_ref[0, 0:1, :] + _dot(kd.astype(bf16), v)
    s_scr[...] = s_new
    so_ref[0, 0] = s_new


def _retention(q, k, v, s0, length):
    Bn, Tn, _ = q.shape
    C = RET_D
    nc = Tn // C
    H = RET_HEADS
    log_g = jnp.log(1.0 - 2.0 ** (-5.0 - jnp.arange(H, dtype=f32)))
    idx = jnp.arange(C, dtype=f32)
    diff = idx[:, None] - idx[None, :]
    dmask = jnp.where(diff >= 0, jnp.exp(jnp.maximum(diff, 0.0) * log_g[:, None, None]), 0.0)
    q_dec = jnp.exp((idx + 1.0) * log_g[:, None])
    k_dec = jnp.where(idx < length, jnp.exp(jnp.maximum(length - 1.0 - idx, 0.0) * log_g[:, None]), 0.0)
    c_dec = jnp.exp(length * log_g)
    qd = jnp.broadcast_to(q_dec[:, :, None], (H, C, C))
    kd = jnp.broadcast_to(k_dec[:, :, None], (H, C, C))
    cd = jnp.broadcast_to(c_dec[:, None, None], (H, 8, C))
    qkv_spec = pl.BlockSpec((1, C, C), lambda b, h, c: (b, c, h))
    tbl_spec = pl.BlockSpec((1, C, C), lambda b, h, c: (h, 0, 0))
    st_spec = pl.BlockSpec((1, 1, C, C), lambda b, h, c: (b, h, 0, 0))
    return pl.pallas_call(
        _retention_kernel,
        out_shape=(jax.ShapeDtypeStruct((Bn, Tn, H * C), f32), jax.ShapeDtypeStruct((Bn, H, C, C), f32)),
        grid=(Bn, H, nc),
        in_specs=[qkv_spec, qkv_spec, qkv_spec, tbl_spec, tbl_spec, tbl_spec,
                  pl.BlockSpec((1, 8, C), lambda b, h, c: (h, 0, 0)), st_spec],
        out_specs=(qkv_spec, st_spec),
        scratch_shapes=[pltpu.VMEM((C, C), f32)],
        compiler_params=_cparams(("parallel", "parallel", "arbitrary")),
        name="retention",
    )(q, k, v, dmask, qd, kd, cd, s0)


def _mla_prompt_kernel(ii_ref, jj_ref, fl_ref, qn_ref, qp_ref, wuk_ref, ckv_ref, kr_ref, wuv_ref, o_ref,
                       qa_s, qp_s, m_s, l_s, acc_s, *, tq, tk):
    p = pl.program_id(1)
    i = ii_ref[p]
    j = jj_ref[p]
    fl = fl_ref[p]
    H = MLA_HEADS

    @pl.when((fl & 1) == 1)
    def _():
        for h in range(H):
            qh = qn_ref[0, :, h * MLA_NOPE:(h + 1) * MLA_NOPE].astype(bf16)
            qa = _dot(qh, wuk_ref[h]) * MLA_SCALE
            qa_s[h * tq:(h + 1) * tq, :] = qa.astype(bf16)
            qp_s[h * tq:(h + 1) * tq, :] = qp_ref[0, :, h * MLA_ROPE:(h + 1) * MLA_ROPE].astype(bf16)
        m_s[...] = jnp.full(m_s.shape, NEG, f32)
        l_s[...] = jnp.zeros(l_s.shape, f32)
        acc_s[...] = jnp.zeros(acc_s.shape, f32)

    kv = ckv_ref[0].astype(bf16)
    kr = kr_ref[0].astype(bf16)
    s = _dot_nt(qa_s[...], kv) + _dot_nt(qp_s[...], kr)
    qpos = i * tq + lax.rem(lax.broadcasted_iota(jnp.int32, s.shape, 0), tq)
    kpos = j * tk + lax.broadcasted_iota(jnp.int32, s.shape, 1)
    s = jnp.where(kpos <= qpos, s, NEG)
    m_prev = m_s[...]
    m_new = jnp.maximum(m_prev, jnp.max(s, axis=-1, keepdims=True))
    alpha = jnp.exp(m_prev - m_new)
    pr = jnp.where(s > 0.5 * NEG, jnp.exp(s - m_new), 0.0)
    l_s[...] = alpha * l_s[...] + jnp.sum(pr, axis=-1, keepdims=True)
    acc_s[...] = alpha * acc_s[...] + _dot(pr.astype(bf16), kv)
    m_s[...] = m_new

    @pl.when((fl & 2) == 2)
    def _():
        o = acc_s[...] / l_s[...]
        for h in range(H):
            o_ref[0, :, h * MLA_V:(h + 1) * MLA_V] = _dot(o[h * tq:(h + 1) * tq].astype(bf16), wuv_ref[h])


def _pairs(nq, jlo, jhi):
    ii, jj, fl = [], [], []
    for i in range(nq):
        lo, hi = jlo(i), jhi(i)
        for j in range(lo, hi + 1):
            ii.append(i)
            jj.append(j)
            fl.append((1 if j == lo else 0) | (2 if j == hi else 0))
    return (jnp.asarray(np.array(ii, np.int32)), jnp.asarray(np.array(jj, np.int32)),
            jnp.asarray(np.array(fl, np.int32)))


def _mla_prompt(qn, qp, ckv, kr, wuk_t, wuv, tq=128, tk=256):
    B, T, _ = qn.shape
    tk = min(tk, T)
    H = MLA_HEADS
    ii, jj, fl = _pairs(T // tq, lambda i: 0, lambda i: (i * tq + tq - 1) // tk)
    npairs = int(ii.shape[0])
    grid_spec = pltpu.PrefetchScalarGridSpec(
        num_scalar_prefetch=3,
        grid=(B, npairs),
        in_specs=[
            pl.BlockSpec((1, tq, H * MLA_NOPE), lambda b, p, ii, jj, fl: (b, ii[p], 0)),
            pl.BlockSpec((1, tq, H * MLA_ROPE), lambda b, p, ii, jj, fl: (b, ii[p], 0)),
            pl.BlockSpec((H, MLA_NOPE, MLA_KV_RANK), lambda b, p, ii, jj, fl: (0, 0, 0)),
            pl.BlockSpec((1, tk, MLA_KV_RANK), lambda b, p, ii, jj, fl: (b, jj[p], 0)),
            pl.BlockSpec((1, tk, MLA_ROPE), lambda b, p, ii, jj, fl: (b, jj[p], 0)),
            pl.BlockSpec((H, MLA_KV_RANK, MLA_V), lambda b, p, ii, jj, fl: (0, 0, 0)),
        ],
        out_specs=pl.BlockSpec((1, tq, H * MLA_V), lambda b, p, ii, jj, fl: (b, ii[p], 0)),
        scratch_shapes=[
            pltpu.VMEM((H * tq, MLA_KV_RANK), bf16),
            pltpu.VMEM((H * tq, MLA_ROPE), bf16),
            pltpu.VMEM((H * tq, 1), f32),
            pltpu.VMEM((H * tq, 1), f32),
            pltpu.VMEM((H * tq, MLA_KV_RANK), f32),
        ],
    )
    return pl.pallas_call(
        functools.partial(_mla_prompt_kernel, tq=tq, tk=tk),
        out_shape=jax.ShapeDtypeStruct((B, T, H * MLA_V), f32),
        grid_spec=grid_spec,
        compiler_params=_cparams(("parallel", "arbitrary")),
        name="mla_prompt",
    )(ii, jj, fl, qn, qp, wuk_t, ckv, kr, wuv)


def _page_copies(pt_ref, cache, buf, sem, layer, npg, seq, slot):
    if len(buf.shape) == 3:
        return [
            pltpu.make_async_copy(cache.at[layer, pt_ref[seq * npg + p]],
                                  buf.at[slot, pl.ds(p * PAGE, PAGE)], sem.at[slot])
            for p in range(npg)
        ]
    return [
        pltpu.make_async_copy(cache.at[layer, pt_ref[seq * npg + p], :, pl.ds(sg * 128, 128)],
                              buf.at[slot, sg, pl.ds(p * PAGE, PAGE)], sem.at[slot])
        for p in range(npg) for sg in range(buf.shape[1])
    ]


def _gather_step(pt_ref, caches, bufs, sems, layer, npg):
    b = pl.program_id(0)
    nb = pl.num_programs(0)

    def start(seq, slot):
        for cache, buf, sem in zip(caches, bufs, sems):
            for cp in _page_copies(pt_ref, cache, buf, sem, layer, npg, seq, slot):
                cp.start()

    @pl.when(b == 0)
    def _():
        start(0, 0)

    @pl.when(b + 1 < nb)
    def _():
        start(b + 1, lax.rem(b + 1, 2))

    slot = lax.rem(b, 2)
    for cache, buf, sem in zip(caches, bufs, sems):
        for cp in _page_copies(pt_ref, cache, buf, sem, layer, npg, b, slot):
            cp.wait()
    return slot


def _paged_attn_kernel(pt_ref, *refs, layer, npg, has2, has_bias, has_mask):
    it = iter(refs)
    q_ref = next(it)
    q2_ref = next(it) if has2 else None
    bias_ref = next(it) if has_bias else None
    mask_ref = next(it) if has_mask else None
    xk_ref = next(it)
    xk2_ref = next(it) if has2 else None
    xb_ref = next(it)
    cache = next(it)
    cache2 = next(it) if has2 else None
    o_ref = next(it)
    buf = next(it)
    buf2 = next(it) if has2 else None
    sem = next(it)
    sem2 = next(it) if has2 else None

    caches, bufs, sems = [cache], [buf], [sem]
    if has2:
        caches.append(cache2)
        bufs.append(buf2)
        sems.append(sem2)
    slot = _gather_step(pt_ref, caches, bufs, sems, layer, npg)

    q = q_ref[0].astype(bf16)
    rows = buf[slot].astype(bf16)
    xk = xk_ref[0].astype(bf16)
    s = _dot_nt(q, rows)
    sx = _dot_nt(q, xk)
    if has2:
        q2 = q2_ref[0].astype(bf16)
        s = s + _dot_nt(q2, buf2[slot].astype(bf16))
        sx = sx + _dot_nt(q2, xk2_ref[0].astype(bf16))
    if has_bias:
        s = s + bias_ref[...]
    if has_mask:
        reps = s.shape[0] // mask_ref.shape[1]
        s = s + jnp.concatenate([mask_ref[0]] * reps, axis=0)
    sx = sx + xb_ref[0]
    m = jnp.maximum(jnp.max(s, axis=-1, keepdims=True), jnp.max(sx, axis=-1, keepdims=True))
    pr = jnp.exp(s - m)
    px = jnp.exp(sx - m)
    l = jnp.sum(pr, axis=-1, keepdims=True) + jnp.sum(px, axis=-1, keepdims=True)
    o = _dot(pr.astype(bf16), rows) + _dot(px.astype(bf16), xk)
    o_ref[0] = jnp.where(m > 0.5 * NEG, o / l, 0.0)


def _paged_attn(page_table, cache, layer, q, xk, xb, q2=None, cache2=None, xk2=None, bias=None, mask=None):
    DB, R, DL = q.shape
    npg = page_table.shape[1]
    K = npg * PAGE
    has2 = q2 is not None
    seq = lambda b, pt: (b, 0, 0)
    in_specs = [pl.BlockSpec((1, R, DL), seq)]
    args = [q]
    if has2:
        d2 = q2.shape[-1]
        in_specs.append(pl.BlockSpec((1, R, d2), seq))
        args.append(q2)
    if bias is not None:
        in_specs.append(pl.BlockSpec((R, K), lambda b, pt: (0, 0)))
        args.append(bias)
    if mask is not None:
        in_specs.append(pl.BlockSpec((1, 8, K), seq))
        args.append(mask)
    in_specs.append(pl.BlockSpec((1, 8, DL), seq))
    args.append(xk)
    if has2:
        in_specs.append(pl.BlockSpec((1, 8, d2), seq))
        args.append(xk2)
    if xb.shape[0] == 1:
        in_specs.append(pl.BlockSpec((1, R, 8), lambda b, pt: (0, 0, 0)))
    else:
        in_specs.append(pl.BlockSpec((1, R, 8), seq))
    args.append(xb)
    in_specs.append(pl.BlockSpec(memory_space=pl.ANY))
    args.append(cache)
    scratch = [pltpu.VMEM((2, K, DL), f32)]
    if has2:
        in_specs.append(pl.BlockSpec(memory_space=pl.ANY))
        args.append(cache2)
        scratch.append(pltpu.VMEM((2, K, d2), f32))
    scratch.append(pltpu.SemaphoreType.DMA((2,)))
    if has2:
        scratch.append(pltpu.SemaphoreType.DMA((2,)))
    grid_spec = pltpu.PrefetchScalarGridSpec(
        num_scalar_prefetch=1,
        grid=(DB,),
        in_specs=in_specs,
        out_specs=pl.BlockSpec((1, R, DL), seq),
        scratch_shapes=scratch,
    )
    return pl.pallas_call(
        functools.partial(_paged_attn_kernel, layer=layer, npg=npg, has2=has2,
                          has_bias=bias is not None, has_mask=mask is not None),
        out_shape=jax.ShapeDtypeStruct((DB, R, DL), f32),
        grid_spec=grid_spec,
        compiler_params=_cparams(("arbitrary",)),
        name="paged_attn",
    )(page_table.reshape(-1), *args)


def _compress_compute(load_rows, pe_ref, w1_ref, w2_ref, o_ref, nblk, W):
    half = nblk // 2
    acc = jnp.zeros((nblk, KV_LANES), f32)
    for r in range(CMP_BLOCK):
        x = jnp.concatenate([load_rows(r, half), load_rows(CMP_BLOCK + r, half)], axis=0) + pe_ref[r:r + 1, :]
        acc = acc + _dot(x.astype(bf16), w1_ref[r])
    hdn = acc * jax.nn.sigmoid(acc)
    out = _dot(hdn.astype(bf16), w2_ref[...])
    o_ref[0, 0:half, :] = out[:half]
    o_ref[0, W:W + half, :] = out[half:]
    if half < W:
        zeros = jnp.zeros((W - half, KV_LANES), f32)
        o_ref[0, half:W, :] = zeros
        o_ref[0, W + half:2 * W, :] = zeros


def _compress_kernel(xlo_ref, xhi_ref, pe_ref, w1_ref, w2_ref, o_ref, *, nblk, W):
    def load(r, n):
        rows = pl.ds(r, n, stride=2 * CMP_BLOCK)
        return jnp.concatenate([xlo_ref[0, rows, :], xhi_ref[0, rows, :]], axis=1)

    _compress_compute(load, pe_ref, w1_ref, w2_ref, o_ref, nblk, W)


def _compress_weight_specs(idx):
    return [
        pl.BlockSpec((CMP_BLOCK, KV_LANES), idx(2)),
        pl.BlockSpec((CMP_BLOCK, KV_LANES, KV_LANES), idx(3)),
        pl.BlockSpec((KV_LANES, KV_LANES), idx(2)),
    ]


def _compress(z3, colblk, pe_row, w1bd, w2p, W):
    B, T, _ = z3.shape
    nblk = T // CMP_BLOCK
    return pl.pallas_call(
        functools.partial(_compress_kernel, nblk=nblk, W=W),
        out_shape=jax.ShapeDtypeStruct((B, 2 * W, KV_LANES), f32),
        grid=(B,),
        in_specs=[pl.BlockSpec((1, T, 128), lambda b: (b, 0, 2 * colblk)),
                  pl.BlockSpec((1, T, 128), lambda b: (b, 0, 2 * colblk + 1))]
        + _compress_weight_specs(lambda n: (lambda b: (0,) * n)),
        out_specs=pl.BlockSpec((1, 2 * W, KV_LANES), lambda b: (b, 0, 0)),
        compiler_params=_cparams(("parallel",)),
        name="compress",
    )(z3, z3, pe_row, w1bd, w2p)


def _compress_paged_kernel(pt_ref, pe_ref, w1_ref, w2_ref, cache, o_ref, buf, sem, *, layer, npg, W):
    slot = _gather_step(pt_ref, [cache], [buf], [sem], layer, npg)

    def load(r, n):
        rows = pl.ds(r, n, stride=2 * CMP_BLOCK)
        return jnp.concatenate([buf[slot, sg, rows, :] for sg in range(buf.shape[1])], axis=1)

    _compress_compute(load, pe_ref, w1_ref, w2_ref, o_ref, npg * PAGE // CMP_BLOCK, W)


def _compress_paged(page_table, cache, layer, pe_row, w1bd, w2p, W):
    DB, npg = page_table.shape
    grid_spec = pltpu.PrefetchScalarGridSpec(
        num_scalar_prefetch=1,
        grid=(DB,),
        in_specs=_compress_weight_specs(lambda n: (lambda b, pt: (0,) * n)) + [pl.BlockSpec(memory_space=pl.ANY)],
        out_specs=pl.BlockSpec((1, 2 * W, KV_LANES), lambda b, pt: (b, 0, 0)),
        scratch_shapes=[pltpu.VMEM((2, KV_LANES // 128, npg * PAGE, 128), f32), pltpu.SemaphoreType.DMA((2,))],
    )
    return pl.pallas_call(
        functools.partial(_compress_paged_kernel, layer=layer, npg=npg, W=W),
        out_shape=jax.ShapeDtypeStruct((DB, 2 * W, KV_LANES), f32),
        grid_spec=grid_spec,
        compiler_params=_cparams(("arbitrary",)),
        name="compress_paged",
    )(page_table.reshape(-1), pe_row, w1bd, w2p, cache)


def _masked_softmax(s):
    valid = s > 0.5 * NEG
    m = jnp.max(s, axis=-1, keepdims=True)
    e = jnp.where(valid, jnp.exp(s - m), 0.0)
    l = jnp.sum(e, axis=-1, keepdims=True)
    return e * jnp.where(l > 0.0, 1.0 / l, 0.0)


def _select_blocks(imp, q_pos, n_blk):
    R, W = imp.shape
    blk = lax.broadcasted_iota(jnp.int32, (R, W), 1)
    cur = q_pos // SEL_BLOCK
    forced = (blk == 0) | (blk == cur) | (blk == cur - 1)
    valid = (blk * SEL_BLOCK <= q_pos) & (blk < n_blk)
    score = jnp.where(valid, jnp.where(forced, FORCE, imp), -1.0)
    score = jnp.where(blk < n_blk, score, -jnp.inf)
    rank = jnp.zeros((R, W), f32)
    for c in range(n_blk):
        col = score[:, c:c + 1]
        beats = (col > score) | ((col == score) & (blk > c))
        rank = rank + jnp.where(beats, 1.0, 0.0)
    return jnp.where((rank < float(min(N_SEL, n_blk))) & valid, 1.0, 0.0)


def _nsa_cmp_prompt_kernel(q_ref, kv_ref, bias_ref, oc_ref, sel_ref, *, tq, W, n_blk):
    i = pl.program_id(2)
    kv = kv_ref[0]
    k = kv[:, :NSA_DH].astype(bf16)
    v = kv[:, NSA_DH:].astype(bf16)
    imp = jnp.zeros((tq, 2 * W), f32)
    for h in range(NSA_HPG):
        qh = (q_ref[0, :, h * NSA_DH:(h + 1) * NSA_DH] * NSA_SCALE).astype(bf16)
        p = _masked_softmax(_dot_nt(qh, k) + bias_ref[h])
        oc_ref[0, :, h * NSA_DH:(h + 1) * NSA_DH] = _dot(p.astype(bf16), v)
        imp = imp + p
    imp = imp[:, :W] + imp[:, W:]
    q_pos = i * tq + lax.broadcasted_iota(jnp.int32, (tq, 1), 0)
    sel_ref[0, 0] = _select_blocks(imp, q_pos, n_blk)


def _nsa_cmp_prompt(z3, kcv, bias, W, n_blk, tq=128):
    B, T, _ = z3.shape
    return pl.pallas_call(
        functools.partial(_nsa_cmp_prompt_kernel, tq=tq, W=W, n_blk=n_blk),
        out_shape=(jax.ShapeDtypeStruct((B, T, NSA_HEADS * NSA_DH), f32),
                   jax.ShapeDtypeStruct((B, NSA_G, T, W), f32)),
        grid=(B, NSA_G, T // tq),
        in_specs=[
            pl.BlockSpec((1, tq, NSA_HPG * NSA_DH), lambda b, g, i: (b, i, g)),
            pl.BlockSpec((1, 2 * W, 2 * NSA_DH), lambda b, g, i: (b, 0, g)),
            pl.BlockSpec((NSA_HPG, tq, 2 * W), lambda b, g, i: (g, i, 0)),
        ],
        out_specs=(pl.BlockSpec((1, tq, NSA_HPG * NSA_DH), lambda b, g, i: (b, i, g)),
                   pl.BlockSpec((1, 1, tq, W), lambda b, g, i: (b, g, i, 0))),
        compiler_params=_cparams(("parallel", "parallel", "parallel")),
        name="nsa_cmp_prompt",
    )(z3, kcv, bias)


def _nsa_cmp_sample_kernel(q_ref, kv_ref, bias_ref, e_ref, oc_ref, sel_ref, mexp_ref, imp_s, *,
                           bt, ts, W, n_blk, past):
    R = ts * NSA_HPG
    for bb in range(bt):
        for g in range(NSA_G):
            k = kv_ref[bb, :, g * 2 * NSA_DH:g * 2 * NSA_DH + NSA_DH].astype(bf16)
            v = kv_ref[bb, :, g * 2 * NSA_DH + NSA_DH:(g + 1) * 2 * NSA_DH].astype(bf16)
            qg = (q_ref[bb, g] * NSA_SCALE).astype(bf16)
            p = _masked_softmax(_dot_nt(qg, k) + bias_ref[g])
            oc_ref[bb, g] = _dot(p.astype(bf16), v)
            imp = jnp.sum(p.reshape(ts, NSA_HPG, 2 * W), axis=1)
            r0 = (bb * NSA_G + g) * ts
            imp_s[r0:r0 + ts, :] = imp[:, :W] + imp[:, W:]
    rows = bt * NSA_G * ts
    q_pos = past + lax.rem(lax.broadcasted_iota(jnp.int32, (rows, 1), 0), ts)
    sel = _select_blocks(imp_s[...], q_pos, n_blk)
    sel_ref[...] = sel.reshape(bt, NSA_G * ts, W)
    hit = _dot(sel.astype(bf16), e_ref[...])
    mexp_ref[...] = jnp.where(hit > 0.5, 0.0, NEG).reshape(bt, NSA_G * ts, mexp_ref.shape[2])


def _nsa_cmp_sample(q, kcv, bias, expand, W, n_blk, past, bt=8):
    DB, _, R, _ = q.shape
    ts = R // NSA_HPG
    K = expand.shape[1]
    rows = NSA_G * ts
    return pl.pallas_call(
        functools.partial(_nsa_cmp_sample_kernel, bt=bt, ts=ts, W=W, n_blk=n_blk, past=past),
        out_shape=(jax.ShapeDtypeStruct((DB, NSA_G, R, NSA_DH), f32),
                   jax.ShapeDtypeStruct((DB, rows, W), f32),
                   jax.ShapeDtypeStruct((DB, rows, K), f32)),
        grid=(DB // bt,),
        in_specs=[
            pl.BlockSpec((bt, NSA_G, R, NSA_DH), lambda b: (b, 0, 0, 0)),
            pl.BlockSpec((bt, 2 * W, KV_LANES), lambda b: (b, 0, 0)),
            pl.BlockSpec((NSA_G, R, 2 * W), lambda b: (0, 0, 0)),
            pl.BlockSpec((W, K), lambda b: (0, 0)),
        ],
        out_specs=(pl.BlockSpec((bt, NSA_G, R, NSA_DH), lambda b: (b, 0, 0, 0)),
                   pl.BlockSpec((bt, rows, W), lambda b: (b, 0, 0)),
                   pl.BlockSpec((bt, rows, K), lambda b: (b, 0, 0))),
        scratch_shapes=[pltpu.VMEM((bt * rows, W), f32)],
        compiler_params=_cparams(("parallel",)),
        name="nsa_cmp_sample",
    )(q, kcv, bias, expand)


def _nsa_flash_kernel(ii_ref, jj_ref, fl_ref, q_ref, kv_ref, tbl_ref, *rest, tq, tk, nd, use_sel):
    if use_sel:
        sel_ref, e_ref, o_ref, q_s, sel_s, m_s, l_s, acc_s = rest
    else:
        o_ref, q_s, m_s, l_s, acc_s = rest
    p = pl.program_id(2)
    i = ii_ref[p]
    j = jj_ref[p]
    fl = fl_ref[p]

    @pl.when((fl & 1) == 1)
    def _():
        q_s[...] = jnp.zeros(q_s.shape, bf16)
        for h in range(NSA_HPG):
            q_s[h * tq:(h + 1) * tq, 0:NSA_DH] = (q_ref[0, :, h * NSA_DH:(h + 1) * NSA_DH] * NSA_SCALE).astype(bf16)
        if use_sel:
            selb = sel_ref[0, 0].astype(bf16)
            for h in range(NSA_HPG):
                sel_s[h * tq:(h + 1) * tq, :] = selb
        m_s[...] = jnp.full(m_s.shape, NEG, f32)
        l_s[...] = jnp.zeros(l_s.shape, f32)
        acc_s[...] = jnp.zeros(acc_s.shape, f32)

    kvb = kv_ref[0].astype(bf16)
    d = jnp.minimum(i - j, nd - 1)
    s = _dot_nt(q_s[...], kvb) + tbl_ref[0, d]
    if use_sel:
        hit = _dot(sel_s[...], e_ref[...])
        s = jnp.where(hit > 0.5, s, NEG)
    m_prev = m_s[...]
    m_new = jnp.maximum(m_prev, jnp.max(s, axis=-1, keepdims=True))
    alpha = jnp.exp(m_prev - m_new)
    pr = jnp.where(s > 0.5 * NEG, jnp.exp(s - m_new), 0.0)
    l_s[...] = alpha * l_s[...] + jnp.sum(pr, axis=-1, keepdims=True)
    acc_s[...] = alpha * acc_s[...] + _dot(pr.astype(bf16), kvb)
    m_s[...] = m_new

    @pl.when((fl & 2) == 2)
    def _():
        l = l_s[...]
        o = acc_s[...] * jnp.where(l > 0.0, 1.0 / l, 0.0)
        for h in range(NSA_HPG):
            o_ref[0, :, h * NSA_DH:(h + 1) * NSA_DH] = o[h * tq:(h + 1) * tq, NSA_DH:]


def _nsa_flash(z3, kvcol, tbl, pairs, sel=None, expand=None, tq=128, tk=128):
    B, T, _ = z3.shape
    ii, jj, fl = pairs
    npairs = int(ii.shape[0])
    nd = tbl.shape[1]
    use_sel = sel is not None
    R = NSA_HPG * tq
    in_specs = [
        pl.BlockSpec((1, tq, NSA_HPG * NSA_DH), lambda b, g, p, ii, jj, fl: (b, ii[p], g)),
        pl.BlockSpec((1, tk, 2 * NSA_DH), lambda b, g, p, ii, jj, fl: (b, jj[p], kvcol + g)),
        pl.BlockSpec((1, nd, R, tk), lambda b, g, p, ii, jj, fl: (g, 0, 0, 0)),
    ]
    args = [z3, z3, tbl]
    scratch = [pltpu.VMEM((R, 2 * NSA_DH), bf16)]
    if use_sel:
        W = sel.shape[-1]
        in_specs += [
            pl.BlockSpec((1, 1, tq, W), lambda b, g, p, ii, jj, fl: (b, g, ii[p], 0)),
            pl.BlockSpec((W, tk), lambda b, g, p, ii, jj, fl: (0, jj[p])),
        ]
        args += [sel, expand]
        scratch.append(pltpu.VMEM((R, W), bf16))
    scratch += [pltpu.VMEM((R, 1), f32), pltpu.VMEM((R, 1), f32), pltpu.VMEM((R, 2 * NSA_DH), f32)]
    grid_spec = pltpu.PrefetchScalarGridSpec(
        num_scalar_prefetch=3,
        grid=(B, NSA_G, npairs),
        in_specs=in_specs,
        out_specs=pl.BlockSpec((1, tq, NSA_HPG * NSA_DH), lambda b, g, p, ii, jj, fl: (b, ii[p], g)),
        scratch_shapes=scratch,
    )
    return pl.pallas_call(
        functools.partial(_nsa_flash_kernel, tq=tq, tk=tk, nd=nd, use_sel=use_sel),
        out_shape=jax.ShapeDtypeStruct((B, T, NSA_HEADS * NSA_DH), f32),
        grid_spec=grid_spec,
        compiler_params=_cparams(("parallel", "parallel", "arbitrary")),
        name="nsa_flash_sel" if use_sel else "nsa_flash_win",
    )(ii, jj, fl, *args)


def _bucket_np(dist):
    n = np.maximum(dist, 0)
    exact = N_BUCKETS // 2
    nf = np.maximum(n, 1).astype(np.float32)
    large = exact + (np.log(nf / exact) / math.log(MAX_DISTANCE / exact) * (N_BUCKETS - exact)).astype(np.int32)
    return np.where(n < exact, n, np.minimum(large, N_BUCKETS - 1)).astype(np.int32)


def _t5_table(rel_bias, dist, mask):
    tbl = jnp.take(rel_bias.astype(f32).T, jnp.asarray(_bucket_np(dist)), axis=1)
    return jnp.where(jnp.asarray(mask), tbl, NEG)


def _cmp_block_of_lane(W, n_cmp):
    lane = np.arange(2 * W)
    half = n_cmp // 2
    blk = np.where(lane < W, 2 * lane, 2 * (lane - W) + 1)
    ok = np.where(lane < W, lane < half, (lane - W) < half)
    return np.where(ok, blk, -1)


def _rope_tables(pos, half):
    inv = ROPE_BASE ** (-jnp.arange(half, dtype=f32) / half)
    ang = pos.astype(f32)[:, None] * inv[None, :]
    return jnp.cos(ang)[:, None, :], jnp.sin(ang)[:, None, :]


def _rope(x, cs):
    cos, sin = cs
    half = x.shape[-1] // 2
    x1, x2 = x[..., :half], x[..., half:]
    return jnp.concatenate([x1 * cos - x2 * sin, x2 * cos + x1 * sin], axis=-1)


def _block_diag(blocks):
    n = len(blocks)
    r, c = blocks[0].shape
    out = jnp.zeros((n * r, n * c), blocks[0].dtype)
    for i, blk in enumerate(blocks):
        out = out.at[i * r:(i + 1) * r, i * c:(i + 1) * c].set(blk)
    return out


def kernel(x_prompt, x_sample, cache_mla_ckv, cache_mla_krope, cache_nsa_cmp, cache_nsa_sel, page_table, state_nsa_win, state_ret, state_ffn_conv, rel_bias, norm_mix, norm_ffn, final_norm, ev_w_in, mla_q_norm, mla_w_qb, mla_kv_norm, mla_w_uk, mla_w_uv, ev_w_out, od_w_in, cmp_pe, cmp_w1, cmp_w2, od_w_out, ffn_w_up, ffn_conv_w, ffn_conv_b, ffn_w_down):
    B, T, D = x_prompt.shape
    DB, TS, _ = x_sample.shape
    depth = norm_mix.shape[0]
    npg = page_table.shape[1]
    past = npg * PAGE
    Mp, Ms = B * T, DB * TS
    pool = cache_mla_ckv.shape[1]
    assert T % 128 == 0 and TS <= 8 and past % (2 * CMP_BLOCK) == 0

    pos_p = jnp.arange(T)
    pos_s = past + jnp.arange(TS)
    pos_all = jnp.concatenate([jnp.tile(pos_p, B), jnp.tile(pos_s, DB)])
    rope16 = _rope_tables(pos_all, MLA_ROPE // 2)
    rope64 = _rope_tables(pos_all, RET_D // 2)

    h = jnp.concatenate([x_prompt.reshape(Mp, D), x_sample.reshape(Ms, D)], axis=0)

    tq = tk = 128
    nq = T // tq
    n_blk_p = T // SEL_BLOCK
    Wp = 128
    assert n_blk_p <= Wp
    n_cmp_p = T // CMP_BLOCK
    lane_blk = _cmp_block_of_lane(Wp, n_cmp_p)
    end = lane_blk * CMP_BLOCK + (CMP_BLOCK - 1)
    dist = np.arange(T)[:, None] - end[None, :]
    bias_cmp_p = _t5_table(rel_bias, dist, (dist >= 0) & (lane_blk >= 0)[None, :])

    def flash_table(nd, window):
        r = np.arange(tq)[:, None]
        c = np.arange(tk)[None, :]
        dd = np.stack([d * tk + r - c for d in range(nd)])
        mask = dd >= 0
        if window:
            mask &= dd < WINDOW
        t = _t5_table(rel_bias, dd, mask)
        return t.reshape(NSA_G, NSA_HPG, nd, tq, tk).transpose(0, 2, 1, 3, 4).reshape(NSA_G, nd, NSA_HPG * tq, tk)

    tbl_sel = flash_table(3, False)
    nd_win = WINDOW // tk + 1
    tbl_win = flash_table(nd_win, True)
    pairs_sel = _pairs(nq, lambda i: 0, lambda i: i)
    pairs_win = _pairs(nq, lambda i: max(i - (nd_win - 1), 0), lambda i: i)
    expand_p = jnp.asarray((np.arange(Wp)[:, None] == (np.arange(T) // SEL_BLOCK)[None, :]).astype(np.float32)).astype(bf16)

    n_blk_s = -(-(past + TS) // SEL_BLOCK)
    Ws = max(128, -(-n_blk_s // 128) * 128)
    n_cmp_s = past // CMP_BLOCK
    assert n_cmp_s // 2 <= Ws
    lane_blk_s = _cmp_block_of_lane(Ws, n_cmp_s)
    end_s = lane_blk_s * CMP_BLOCK + (CMP_BLOCK - 1)
    qpos_s = past + np.arange(TS)
    dist_s = qpos_s[:, None] - end_s[None, :]
    bias_cmp_s = _t5_table(rel_bias, dist_s, (dist_s >= 0) & (lane_blk_s >= 0)[None, :])
    bias_cmp_s = bias_cmp_s.reshape(NSA_G, NSA_HPG, TS, 2 * Ws).transpose(0, 2, 1, 3).reshape(NSA_G, TS * NSA_HPG, 2 * Ws)
    expand_s = jnp.asarray((np.arange(Ws)[:, None] == (np.arange(past) // SEL_BLOCK)[None, :]).astype(np.float32)).astype(bf16)

    def sample_rows_table(kpos, window):
        dd = qpos_s[:, None] - kpos[None, :]
        mask = dd >= 0
        if window:
            mask &= (dd < WINDOW) & (kpos >= 0)[None, :]
        t = _t5_table(rel_bias, dd, mask)
        t = t.reshape(NSA_G, NSA_HPG, TS, -1).transpose(1, 0, 2, 3)
        return t.reshape(NSA_HPG * NSA_G * TS, -1)

    bias_sel_s = sample_rows_table(np.arange(past), False)
    wb = state_nsa_win.shape[2]
    assert wb % PAGE == 0
    bias_win_s = sample_rows_table(past - wb + np.arange(wb), True)
    new_pos = np.concatenate([qpos_s, np.full(8 - TS, 10 ** 9)])
    xb_nsa = sample_rows_table(new_pos, False)
    win_pages = jnp.arange(DB * (wb // PAGE), dtype=jnp.int32).reshape(DB, wb // PAGE)

    tt = np.repeat(np.arange(TS), MLA_HEADS)
    xb_mla = jnp.asarray(np.where(np.arange(8)[None, :] <= tt[:, None], 0.0, NEG).astype(np.float32))[None]

    outs = {k: [] for k in ("ckv_p", "ckv_s", "kr_p", "kr_s", "cmp_p", "cmp_s", "sel_p", "sel_s",
                            "win_p", "win_s", "ret_p", "ret_s", "conv_p", "conv_s")}

    def split(a):
        return a[:Mp], a[Mp:]

    def pad_rows(a, n):
        return jnp.pad(a, ((0, 0), (0, n - a.shape[1]), (0, 0)))

    for layer in range(depth):
        j = layer // 2
        if layer % 2 == 0:
            w = ev_w_in[j]
            a0 = MLA_Q_RANK + MLA_KV_RANK
            c0 = a0 + MLA_ROPE
            w_p = jnp.concatenate([w[:, :a0], w[:, c0:], w[:, a0:c0], jnp.zeros((D, 96), f32)], axis=1).astype(bf16)
            z = _mm(h, w_p, g=norm_mix[layer])
            q = _mm(z, mla_w_qb[j].astype(bf16), g=mla_q_norm[j])
            zc = z[:, MLA_Q_RANK:a0]
            ckv = zc * lax.rsqrt(jnp.mean(zc * zc, axis=-1, keepdims=True) + EPS) * mla_kv_norm[j]
            r0 = a0
            nr = RET_HEADS * RET_D
            kr = _rope(z[:, r0 + 4 * nr:r0 + 4 * nr + MLA_ROPE][:, None, :], rope16)[:, 0]
            q3 = q.reshape(-1, MLA_HEADS, MLA_NOPE + MLA_ROPE)
            qn = q3[..., :MLA_NOPE].reshape(-1, MLA_HEADS * MLA_NOPE)
            qp = (_rope(q3[..., MLA_NOPE:], rope16) * MLA_SCALE).reshape(-1, MLA_HEADS * MLA_ROPE)
            rq = _rope(z[:, r0:r0 + nr].reshape(-1, RET_HEADS, RET_D), rope64).reshape(-1, nr)
            rk = (_rope(z[:, r0 + nr:r0 + 2 * nr].reshape(-1, RET_HEADS, RET_D), rope64) * RET_D ** -0.5).reshape(-1, nr)
            rv = z[:, r0 + 2 * nr:r0 + 3 * nr]
            rg = z[:, r0 + 3 * nr:r0 + 4 * nr]

            wuk_t = mla_w_uk[j].transpose(1, 2, 0).astype(bf16)
            wuv = mla_w_uv[j].transpose(1, 0, 2).astype(bf16)
            qn_p, qn_s = split(qn)
            qp_p, qp_s = split(qp)
            ckv_p, ckv_s = split(ckv)
            kr_p, kr_s = split(kr)

            o_mla_p = _mla_prompt(qn_p.reshape(B, T, -1), qp_p.reshape(B, T, -1), ckv_p.reshape(B, T, -1),
                                  kr_p.reshape(B, T, -1), wuk_t, wuv).reshape(Mp, -1)
            wuk_bd = _block_diag([wuk_t[hh] for hh in range(MLA_HEADS)])
            qa_s = (_mm(qn_s, wuk_bd) * MLA_SCALE).reshape(DB, TS * MLA_HEADS, MLA_KV_RANK)
            o_lat = _paged_attn(page_table, cache_mla_ckv, j, qa_s,
                                pad_rows(ckv_s.reshape(DB, TS, -1), 8), xb_mla,
                                q2=qp_s.reshape(DB, TS * MLA_HEADS, MLA_ROPE), cache2=cache_mla_krope,
                                xk2=pad_rows(kr_s.reshape(DB, TS, -1), 8))
            wuv_bd = _block_diag([wuv[hh] for hh in range(MLA_HEADS)])
            o_mla_s = _mm(o_lat.reshape(Ms, MLA_HEADS * MLA_KV_RANK), wuv_bd)

            rq_p, rq_s = split(rq)
            rk_p, rk_s = split(rk)
            rv_p, rv_s = split(rv)
            ro_p, rs_p = _retention(rq_p.reshape(B, T, nr), rk_p.reshape(B, T, nr), rv_p.reshape(B, T, nr),
                                    jnp.zeros((B, RET_HEADS, RET_D, RET_D), f32), float(min(RET_D, T)))
            ro_s, rs_s = _retention(pad_rows(rq_s.reshape(DB, TS, nr), RET_D), pad_rows(rk_s.reshape(DB, TS, nr), RET_D),
                                    pad_rows(rv_s.reshape(DB, TS, nr), RET_D), state_ret[j], float(TS))
            ro = jnp.concatenate([ro_p.reshape(Mp, nr), ro_s[:, :TS].reshape(Ms, nr)], axis=0)
            ro = ro.reshape(-1, RET_HEADS, RET_D)
            rn = ro * lax.rsqrt(jnp.mean(ro * ro, axis=-1, keepdims=True) + EPS)
            o_ret = jax.nn.silu(rg) * rn.reshape(-1, nr)
            mix = jnp.concatenate([jnp.concatenate([o_mla_p, o_mla_s], axis=0), o_ret], axis=1)
            h = _mm(mix, ev_w_out[j].astype(bf16), resid=h)

            outs["ckv_p"].append(ckv_p.reshape(B, T, -1))
            outs["ckv_s"].append(ckv_s.reshape(DB, TS, -1))
            outs["kr_p"].append(kr_p.reshape(B, T, -1))
            outs["kr_s"].append(kr_s.reshape(DB, TS, -1))
            outs["ret_p"].append(rs_p)
            outs["ret_s"].append(rs_s)
        else:
            w = od_w_in[j]
            nq_cols = NSA_HEADS * NSA_DH
            wkv = w[:, nq_cols:nq_cols + 3 * KV_LANES].reshape(D, 3, 2, NSA_G, NSA_DH)
            wsel = wkv[:, 1].transpose(0, 2, 1, 3).reshape(D, KV_LANES)
            wwin = wkv[:, 2].transpose(0, 2, 1, 3).reshape(D, KV_LANES)
            ngate = 3 * NSA_HEADS
            w_p = jnp.concatenate([w[:, :nq_cols + KV_LANES], wsel, wwin, w[:, nq_cols + 3 * KV_LANES:],
                                   jnp.zeros((D, 128 - ngate), f32)], axis=1).astype(bf16)
            z = _mm(h, w_p, g=norm_mix[layer])
            NZ = z.shape[1]
            c_cmp = nq_cols
            c_sel = c_cmp + KV_LANES
            c_win = c_sel + KV_LANES
            c_gate = c_win + KV_LANES
            z_p, z_s = split(z)
            z3 = z_p.reshape(B, T, NZ)
            gates = jax.nn.sigmoid(z[:, c_gate:c_gate + ngate]).reshape(-1, NSA_HEADS, 3)

            def orig_kv(a):
                return a.reshape(a.shape[0], NSA_G, 2, NSA_DH).transpose(0, 2, 1, 3)

            cmp_rows = z[:, c_cmp:c_sel].reshape(-1, 2, NSA_G, NSA_DH)
            sel_rows = orig_kv(z[:, c_sel:c_win])
            win_rows = orig_kv(z[:, c_win:c_gate])

            w1k = cmp_w1[j, 0].reshape(CMP_BLOCK, NSA_DH, -1)
            w1v = cmp_w1[j, 1].reshape(CMP_BLOCK, NSA_DH, -1)
            w1bd = jnp.stack([_block_diag([w1k[r], w1k[r], w1v[r], w1v[r]]) for r in range(CMP_BLOCK)]).astype(bf16)
            w2k, w2v = cmp_w2[j, 0], cmp_w2[j, 1]
            zb = jnp.zeros_like(w2k)
            w2p = jnp.concatenate([
                jnp.concatenate([w2k, zb, zb, zb], axis=1),
                jnp.concatenate([zb, zb, w2k, zb], axis=1),
                jnp.concatenate([zb, w2v, zb, zb], axis=1),
                jnp.concatenate([zb, zb, zb, w2v], axis=1)], axis=0).astype(bf16)
            pe_row = jnp.concatenate([cmp_pe[j, 0], cmp_pe[j, 0], cmp_pe[j, 1], cmp_pe[j, 1]], axis=1)

            kcv_p = _compress(z3, c_cmp // KV_LANES, pe_row, w1bd, w2p, Wp)
            oc_p, sel_p = _nsa_cmp_prompt(z3, kcv_p, bias_cmp_p, Wp, n_blk_p)
            os_p = _nsa_flash(z3, c_sel // 128, tbl_sel, pairs_sel, sel=sel_p, expand=expand_p)
            ow_p = _nsa_flash(z3, c_win // 128, tbl_win, pairs_win)

            cache_cmp = cache_nsa_cmp.reshape(cache_nsa_cmp.shape[0], pool, PAGE, KV_LANES)
            cache_sel = cache_nsa_sel.reshape(cache_nsa_sel.shape[0], pool, PAGE, KV_LANES)
            kcv_s = _compress_paged(page_table, cache_cmp, j, pe_row, w1bd, w2p, Ws)
            q_s5 = z_s[:, :nq_cols].reshape(DB, TS, NSA_G, NSA_HPG, NSA_DH)
            q_tj = q_s5.transpose(0, 2, 1, 3, 4).reshape(DB, NSA_G, TS * NSA_HPG, NSA_DH)
            oc_s, sel_s, mexp_s = _nsa_cmp_sample(q_tj, kcv_s, bias_cmp_s, expand_s, Ws, n_blk_s, past)
            oc_s = oc_s.reshape(DB, NSA_G, TS, NSA_HPG, NSA_DH).transpose(0, 2, 1, 3, 4)
            q_jgt = q_s5.transpose(0, 3, 2, 1, 4) * NSA_SCALE
            lane_g = (jnp.arange(KV_LANES) // NSA_DH)[None, None, None, None, :] == jnp.arange(NSA_G)[None, None, :, None, None]
            q_aug = jnp.where(lane_g, jnp.tile(q_jgt, (1, 1, 1, 1, KV_LANES // NSA_DH)), 0.0)
            q_aug = q_aug.reshape(DB, NSA_HPG * NSA_G * TS, KV_LANES)
            z_s3 = z_s.reshape(DB, TS, NZ)
            new_sel = sel_rows[Mp:].reshape(DB, TS, KV_LANES)
            new_win = win_rows[Mp:].reshape(DB, TS, KV_LANES)
            cur_blk = past // SEL_BLOCK
            member = sel_s[:, :, cur_blk].reshape(DB, 1, NSA_G * TS)
            member = jnp.broadcast_to(member, (DB, NSA_HPG, NSA_G * TS)).reshape(DB, -1, 1)
            xb_sel = jnp.where(member > 0.5, xb_nsa[None], NEG)
            o_sel = _paged_attn(page_table, cache_sel, j, q_aug, pad_rows(new_sel, 8), xb_sel,
                                bias=bias_sel_s, mask=mexp_s)
            win_cache = state_nsa_win.reshape(state_nsa_win.shape[0], DB * (wb // PAGE), PAGE, KV_LANES)
            o_win = _paged_attn(win_pages, win_cache, j, q_aug, pad_rows(new_win, 8), xb_nsa[None],
                                bias=bias_win_s)

            def unpack(o):
                o = o.reshape(DB, NSA_HPG, NSA_G, TS, 2, NSA_G, NSA_DH)[:, :, :, :, 1]
                o = jnp.stack([o[:, :, g, :, g] for g in range(NSA_G)], axis=2)
                return o.transpose(0, 3, 2, 1, 4)

            os_s, ow_s = unpack(o_sel), unpack(o_win)

            def cat(p_, s_):
                return jnp.concatenate([p_.reshape(Mp, NSA_HEADS, NSA_DH), s_.reshape(Ms, NSA_HEADS, NSA_DH)], axis=0)

            o = (gates[..., 0:1] * cat(oc_p, oc_s) + gates[..., 1:2] * cat(os_p, os_s)
                 + gates[..., 2:3] * cat(ow_p, ow_s)).reshape(-1, nq_cols)
            h = _mm(o, od_w_out[j].astype(bf16), resid=h)

            wrows = min(WINDOW, T)
            outs["cmp_p"].append(cmp_rows[:Mp].reshape(B, T, 2, NSA_G, NSA_DH))
            outs["cmp_s"].append(cmp_rows[Mp:].reshape(DB, TS, 2, NSA_G, NSA_DH))
            outs["sel_p"].append(sel_rows[:Mp].reshape(B, T, 2, NSA_G, NSA_DH))
            outs["sel_s"].append(sel_rows[Mp:].reshape(DB, TS, 2, NSA_G, NSA_DH))
            outs["win_p"].append(win_rows[:Mp].reshape(B, T, 2, NSA_G, NSA_DH)[:, T - wrows:])
            win_cat = jnp.concatenate([state_nsa_win[j], win_rows[Mp:].reshape(DB, TS, 2, NSA_G, NSA_DH)], axis=1)
            outs["win_s"].append(win_cat[:, TS:])

        up = _mm(h, ffn_w_up[layer].astype(bf16), g=norm_ffn[layer])
        a_p = up[:Mp, :D_FF].reshape(B, T, D_FF)
        a_s = up[Mp:, :D_FF].reshape(DB, TS, D_FF)
        a_ext_s = jnp.concatenate([state_ffn_conv[layer], a_s], axis=1)
        a1 = jnp.concatenate([jnp.pad(a_p[:, :-1], ((0, 0), (1, 0), (0, 0))).reshape(Mp, D_FF),
                              a_ext_s[:, 1:1 + TS].reshape(Ms, D_FF)], axis=0)
        a2 = jnp.concatenate([jnp.pad(a_p[:, :-2], ((0, 0), (2, 0), (0, 0))).reshape(Mp, D_FF),
                              a_ext_s[:, 0:TS].reshape(Ms, D_FF)], axis=0)
        h = _ffn_down(up, a1, a2, ffn_conv_w[layer], ffn_conv_b[layer], ffn_w_down[layer].astype(bf16), h)
        outs["conv_p"].append(a_p[:, T - 2:])
        outs["conv_s"].append(a_ext_s[:, TS:])

    y = _rmsnorm(h, final_norm)
    st = lambda k: jnp.stack(outs[k])
    return (y[:Mp].reshape(B, T, D), y[Mp:].reshape(DB, TS, D),
            st("ckv_p"), st("ckv_s"), st("kr_p"), st("kr_s"),
            st("cmp_p"), st("cmp_s"), st("sel_p"), st("sel_s"),
            st("win_p"), st("win_s"), st("ret_p"), st("ret_s"),
            st("conv_p"), st("conv_s"))
```

```python
import functools
import math

import numpy as np
import jax
import jax.numpy as jnp
from jax import lax
from jax.experimental import pallas as pl
from jax.experimental.pallas import tpu as pltpu

f32 = jnp.float32
bf16 = jnp.bfloat16

PAGE = 128
MLA_HEADS = 8
MLA_Q_RANK = 384
MLA_KV_RANK = 256
MLA_NOPE = 64
MLA_ROPE = 32
MLA_V = 64
MLA_SCALE = (MLA_NOPE + MLA_ROPE) ** -0.5
RET_HEADS = 4
RET_D = 128
NSA_HEADS = 16
NSA_G = 2
NSA_HPG = 8
NSA_DH = 64
NSA_SCALE = NSA_DH ** -0.5
CMP_BLOCK = 32
SEL_BLOCK = 64
N_SEL = 16
WINDOW = 512
D_FF = 2816
N_BUCKETS = 32
MAX_DISTANCE = 128
ROPE_BASE = 10000.0
EPS = 1e-6
NEG = -1e30
FORCE = 1e4
KV_LANES = 2 * NSA_G * NSA_DH
LANES = 128
FLASH_TK = 512

VMEM_LIMIT = 56 * 1024 * 1024


def _cparams(sem):
    return pltpu.CompilerParams(dimension_semantics=sem, vmem_limit_bytes=VMEM_LIMIT)


def _dot(a, b):
    return jnp.dot(a, b, preferred_element_type=f32)


def _dot_nt(a, b):
    return lax.dot_general(a, b, (((1,), (1,)), ((), ())), preferred_element_type=f32)


def _row_tile(m, cap):
    for t in (512, 256, 128, 64, 32, 16, 8):
        if t <= cap and m % t == 0:
            return t
    return m


def _pick_tile(n, cap):
    best = None
    for t in range(128, min(n, cap) + 1, 128):
        if n % t == 0:
            best = t
    return best if best is not None else n


def _mm_kernel(*refs, norm, resid):
    it = iter(refs)
    x_ref = next(it)
    g_ref = next(it) if norm else None
    w_ref = next(it)
    r_ref = next(it) if resid else None
    o_ref = next(it)
    xn_ref = next(it)

    @pl.when(pl.program_id(1) == 0)
    def _():
        x = x_ref[...]
        if norm:
            x = x * lax.rsqrt(jnp.mean(x * x, axis=-1, keepdims=True) + EPS) * g_ref[...]
        xn_ref[...] = x.astype(bf16)

    acc = _dot(xn_ref[...], w_ref[...])
    if resid:
        acc = acc + r_ref[...]
    o_ref[...] = acc


def _mm(x, w, g=None, resid=None, tm=512):
    M = x.shape[0]
    K, N = w.shape
    tm = _row_tile(M, tm)
    assert K == x.shape[1] or K % 128 == 0
    tn = _pick_tile(N, 1536)
    norm = g is not None
    has_r = resid is not None
    in_specs = [pl.BlockSpec((tm, K), lambda i, j: (i, 0))]
    args = [x]
    if norm:
        in_specs.append(pl.BlockSpec((1, K), lambda i, j: (0, 0)))
        args.append(g.reshape(1, K).astype(f32))
    in_specs.append(pl.BlockSpec((K, tn), lambda i, j: (0, j)))
    args.append(w)
    if has_r:
        in_specs.append(pl.BlockSpec((tm, tn), lambda i, j: (i, j)))
        args.append(resid)
    return pl.pallas_call(
        functools.partial(_mm_kernel, norm=norm, resid=has_r),
        out_shape=jax.ShapeDtypeStruct((M, N), f32),
        grid=(M // tm, N // tn),
        in_specs=in_specs,
        out_specs=pl.BlockSpec((tm, tn), lambda i, j: (i, j)),
        scratch_shapes=[pltpu.VMEM((tm, K), bf16)],
        compiler_params=_cparams(("parallel", "arbitrary")),
        name="mm",
    )(*args)


def _rmsnorm_kernel(x_ref, g_ref, o_ref):
    x = x_ref[...]
    o_ref[...] = x * lax.rsqrt(jnp.mean(x * x, axis=-1, keepdims=True) + EPS) * g_ref[...]


def _rmsnorm(x, g, tm=512):
    M, D = x.shape
    tm = _row_tile(M, tm)
    return pl.pallas_call(
        _rmsnorm_kernel,
        out_shape=jax.ShapeDtypeStruct((M, D), f32),
        grid=(M // tm,),
        in_specs=[pl.BlockSpec((tm, D), lambda i: (i, 0)), pl.BlockSpec((1, D), lambda i: (0, 0))],
        out_specs=pl.BlockSpec((tm, D), lambda i: (i, 0)),
        compiler_params=_cparams(("parallel",)),
        name="rmsnorm",
    )(x, g.reshape(1, D))


def _ffn_gate_down(a, a1, a2, u, cw_ref, cb_ref, w_ref, r_ref, o_ref):
    cw = cw_ref[...]
    c = cb_ref[...] + a2 * cw[0:1] + a1 * cw[1:2] + a * cw[2:3]
    gate = c * jax.nn.sigmoid(c) * u
    o_ref[...] = _dot(gate.astype(bf16), w_ref[...]) + r_ref[...]


def _ffn_down_prompt_kernel(a_ref, prev_ref, u_ref, cw_ref, cb_ref, w_ref, r_ref, o_ref, *, tiles_per_seq):
    a = a_ref[...]
    keep = jnp.where(lax.rem(pl.program_id(0), tiles_per_seq) == 0, 0.0, 1.0)
    p7 = prev_ref[7:8, :] * keep
    p6 = prev_ref[6:7, :] * keep
    row = lax.broadcasted_iota(jnp.int32, a.shape, 0)
    a1 = jnp.where(row == 0, p7, pltpu.roll(a, 1, 0))
    a2 = jnp.where(row == 0, p6, jnp.where(row == 1, p7, pltpu.roll(a, 2, 0)))
    _ffn_gate_down(a, a1, a2, u_ref[...], cw_ref, cb_ref, w_ref, r_ref, o_ref)


def _ffn_down_prompt(up, conv_w, conv_b, w_down, resid, Mp, T, tm=256):
    D = w_down.shape[1]
    tm = _row_tile(T, tm)
    row = lambda i: (i, 0)
    return pl.pallas_call(
        functools.partial(_ffn_down_prompt_kernel, tiles_per_seq=T // tm),
        out_shape=jax.ShapeDtypeStruct((Mp, D), f32),
        grid=(Mp // tm,),
        in_specs=[
            pl.BlockSpec((tm, D_FF), row),
            pl.BlockSpec((8, D_FF), lambda i: (jnp.maximum(i * (tm // 8) - 1, 0), 0)),
            pl.BlockSpec((tm, D_FF), lambda i: (i, 1)),
            pl.BlockSpec((3, D_FF), lambda i: (0, 0)),
            pl.BlockSpec((1, D_FF), lambda i: (0, 0)),
            pl.BlockSpec((D_FF, D), lambda i: (0, 0)),
            pl.BlockSpec((tm, D), row),
        ],
        out_specs=pl.BlockSpec((tm, D), row),
        compiler_params=_cparams(("parallel",)),
        name="ffn_down_prompt",
    )(up, up, up, conv_w, conv_b.reshape(1, D_FF), w_down, resid)


def _ffn_down_rows_kernel(a_ref, a1_ref, a2_ref, u_ref, cw_ref, cb_ref, w_ref, r_ref, o_ref):
    _ffn_gate_down(a_ref[...], a1_ref[...], a2_ref[...], u_ref[...], cw_ref, cb_ref, w_ref, r_ref, o_ref)


def _ffn_down_rows(up, a1, a2, conv_w, conv_b, w_down, resid, row0, tm=256):
    Ms = a1.shape[0]
    D = w_down.shape[1]
    tm = _row_tile(math.gcd(Ms, row0) if row0 else Ms, tm)
    off = row0 // tm
    row = lambda i: (i, 0)
    return pl.pallas_call(
        _ffn_down_rows_kernel,
        out_shape=jax.ShapeDtypeStruct((Ms, D), f32),
        grid=(Ms // tm,),
        in_specs=[
            pl.BlockSpec((tm, D_FF), lambda i: (i + off, 0)),
            pl.BlockSpec((tm, D_FF), row),
            pl.BlockSpec((tm, D_FF), row),
            pl.BlockSpec((tm, D_FF), lambda i: (i + off, 1)),
            pl.BlockSpec((3, D_FF), lambda i: (0, 0)),
            pl.BlockSpec((1, D_FF), lambda i: (0, 0)),
            pl.BlockSpec((D_FF, D), lambda i: (0, 0)),
            pl.BlockSpec((tm, D), lambda i: (i + off, 0)),
        ],
        out_specs=pl.BlockSpec((tm, D), row),
        compiler_params=_cparams(("parallel",)),
        name="ffn_down_rows",
    )(up, a1, a2, up, conv_w, conv_b.reshape(1, D_FF), w_down, resid)


def _retention_kernel(q_ref, k_ref, v_ref, dm_ref, qd_ref, kd_ref, cd_ref, s0_ref, o_ref, so_ref, s_scr):
    c = pl.program_id(2)

    @pl.when(c == 0)
    def _():
        s_scr[...] = s0_ref[0, 0]

    q = q_ref[0]
    k = k_ref[0]
    v = v_ref[0].astype(bf16)
    S = s_scr[...]
    inner = _dot_nt(q.astype(bf16), k.astype(bf16)) * dm_ref[0]
    o = _dot(inner.astype(bf16), v) + _dot((q * qd_ref[0]).astype(bf16), S.astype(bf16))
    o_ref[0] = o
    kd = (k * kd_ref[0]).T
    s_new = S * cd_ref[0, 0:1, :] + _dot(kd.astype(bf16), v)
    s_scr[...] = s_new
    so_ref[0, 0] = s_new


def _retention(q, k, v, s0, length):
    Bn, Tn, _ = q.shape
    C = RET_D
    nc = Tn // C
    H = RET_HEADS
    log_g = jnp.log(1.0 - 2.0 ** (-5.0 - jnp.arange(H, dtype=f32)))
    idx = jnp.arange(C, dtype=f32)
    diff = idx[:, None] - idx[None, :]
    dmask = jnp.where(diff >= 0, jnp.exp(jnp.maximum(diff, 0.0) * log_g[:, None, None]), 0.0)
    q_dec = jnp.exp((idx + 1.0) * log_g[:, None])
    k_dec = jnp.where(idx < length, jnp.exp(jnp.maximum(length - 1.0 - idx, 0.0) * log_g[:, None]), 0.0)
    c_dec = jnp.exp(length * log_g)
    qd = jnp.broadcast_to(q_dec[:, :, None], (H, C, C))
    kd = jnp.broadcast_to(k_dec[:, :, None], (H, C, C))
    cd = jnp.broadcast_to(c_dec[:, None, None], (H, 8, C))
    qkv_spec = pl.BlockSpec((1, C, C), lambda b, h, c: (b, c, h))
    tbl_spec = pl.BlockSpec((1, C, C), lambda b, h, c: (h, 0, 0))
    st_spec = pl.BlockSpec((1, 1, C, C), lambda b, h, c: (b, h, 0, 0))
    return pl.pallas_call(
        _retention_kernel,
        out_shape=(jax.ShapeDtypeStruct((Bn, Tn, H * C), f32), jax.ShapeDtypeStruct((Bn, H, C, C), f32)),
        grid=(Bn, H, nc),
        in_specs=[qkv_spec, qkv_spec, qkv_spec, tbl_spec, tbl_spec, tbl_spec,
                  pl.BlockSpec((1, 8, C), lambda b, h, c: (h, 0, 0)), st_spec],
        out_specs=(qkv_spec, st_spec),
        scratch_shapes=[pltpu.VMEM((C, C), f32)],
        compiler_params=_cparams(("parallel", "parallel", "arbitrary")),
        name="retention",
    )(q, k, v, dmask, qd, kd, cd, s0)


def _pairs(nq, jlo, jhi):
    ii, jj, fl = [], [], []
    for i in range(nq):
        lo, hi = jlo(i), jhi(i)
        for j in range(lo, hi + 1):
            ii.append(i)
            jj.append(j)
            fl.append((1 if j == lo else 0) | (2 if j == hi else 0))
    return (jnp.asarray(np.array(ii, np.int32)), jnp.asarray(np.array(jj, np.int32)),
            jnp.asarray(np.array(fl, np.int32)))


def _add_bias_tiles(s, tbl_ref, i, tile0, nd):
    parts = []
    for c in range(s.shape[1] // LANES):
        d = i - (tile0 + c)
        idx = jnp.where(d < 0, nd, jnp.minimum(d, nd - 1))
        parts.append(s[:, c * LANES:(c + 1) * LANES] + tbl_ref[0, idx])
    return jnp.concatenate(parts, axis=1)


def _flash_init(m_s, l_s, acc_s):
    m_s[...] = jnp.full(m_s.shape, NEG, f32)
    l_s[...] = jnp.zeros(l_s.shape, f32)
    acc_s[...] = jnp.zeros(acc_s.shape, f32)


def _flash_update(s, v, m_s, l_s, acc_s):
    m_prev = m_s[...]
    m_new = jnp.maximum(m_prev, jnp.max(s, axis=-1, keepdims=True))
    alpha = jnp.exp(m_prev - m_new)
    pr = jnp.exp(s - m_new)
    l_s[...] = alpha * l_s[...] + jnp.sum(pr, axis=-1, keepdims=True)
    acc_s[...] = alpha * acc_s[...] + _dot(pr.astype(bf16), v)
    m_s[...] = m_new


def _mla_prompt_kernel(ii_ref, jj_ref, fl_ref, qn_ref, qp_ref, wuk_ref, ckv_ref, kr_ref, tbl_ref, wuv_ref, o_ref,
                       qa_s, qp_s, m_s, l_s, acc_s, *, tq, tk):
    p = pl.program_id(1)
    i = ii_ref[p]
    j = jj_ref[p]
    fl = fl_ref[p]
    H = MLA_HEADS

    @pl.when((fl & 1) == 1)
    def _():
        for h in range(H):
            qh = qn_ref[0, :, h * MLA_NOPE:(h + 1) * MLA_NOPE].astype(bf16)
            qa = _dot(qh, wuk_ref[h]) * MLA_SCALE
            qa_s[h * tq:(h + 1) * tq, :] = qa.astype(bf16)
            qp_s[h * tq:(h + 1) * tq, :] = qp_ref[0, :, h * MLA_ROPE:(h + 1) * MLA_ROPE].astype(bf16)
        _flash_init(m_s, l_s, acc_s)

    kv = ckv_ref[0].astype(bf16)
    kr = kr_ref[0].astype(bf16)
    s = _dot_nt(qa_s[...], kv) + _dot_nt(qp_s[...], kr)
    s = _add_bias_tiles(s, tbl_ref, i, j * (tk // LANES), tbl_ref.shape[1] - 1)
    _flash_update(s, kv, m_s, l_s, acc_s)

    @pl.when((fl & 2) == 2)
    def _():
        o = acc_s[...] / l_s[...]
        for h in range(H):
            o_ref[0, :, h * MLA_V:(h + 1) * MLA_V] = _dot(o[h * tq:(h + 1) * tq].astype(bf16), wuv_ref[h])


def _mla_prompt(qn, qp, ckv, kr, wuk_t, wuv, tk=FLASH_TK):
    B, T, _ = qn.shape
    tq = LANES
    tk = min(tk, T)
    H = MLA_HEADS
    R = H * tq
    ii, jj, fl = _pairs(T // tq, lambda i: 0, lambda i: (i * tq + tq - 1) // tk)
    npairs = int(ii.shape[0])
    r = np.arange(tq)[:, None]
    c = np.arange(LANES)[None, :]
    causal = np.where(c <= r, 0.0, NEG).astype(np.float32)
    tbl = np.stack([np.tile(causal, (H, 1)), np.zeros((R, LANES), np.float32), np.full((R, LANES), NEG, np.float32)])
    grid_spec = pltpu.PrefetchScalarGridSpec(
        num_scalar_prefetch=3,
        grid=(B, npairs),
        in_specs=[
            pl.BlockSpec((1, tq, H * MLA_NOPE), lambda b, p, ii, jj, fl: (b, ii[p], 0)),
            pl.BlockSpec((1, tq, H * MLA_ROPE), lambda b, p, ii, jj, fl: (b, ii[p], 0)),
            pl.BlockSpec((H, MLA_NOPE, MLA_KV_RANK), lambda b, p, ii, jj, fl: (0, 0, 0)),
            pl.BlockSpec((1, tk, MLA_KV_RANK), lambda b, p, ii, jj, fl: (b, jj[p], 0)),
            pl.BlockSpec((1, tk, MLA_ROPE), lambda b, p, ii, jj, fl: (b, jj[p], 0)),
            pl.BlockSpec((1, 3, R, LANES), lambda b, p, ii, jj, fl: (0, 0, 0, 0)),
            pl.BlockSpec((H, MLA_KV_RANK, MLA_V), lambda b, p, ii, jj, fl: (0, 0, 0)),
        ],
        out_specs=pl.BlockSpec((1, tq, H * MLA_V), lambda b, p, ii, jj, fl: (b, ii[p], 0)),
        scratch_shapes=[
            pltpu.VMEM((R, MLA_KV_RANK), bf16),
            pltpu.VMEM((R, MLA_ROPE), bf16),
            pltpu.VMEM((R, 1), f32),
            pltpu.VMEM((R, 1), f32),
            pltpu.VMEM((R, MLA_KV_RANK), f32),
        ],
    )
    return pl.pallas_call(
        functools.partial(_mla_prompt_kernel, tq=tq, tk=tk),
        out_shape=jax.ShapeDtypeStruct((B, T, H * MLA_V), f32),
        grid_spec=grid_spec,
        compiler_params=_cparams(("parallel", "arbitrary")),
        name="mla_prompt",
    )(ii, jj, fl, qn, qp, wuk_t, ckv, kr, jnp.asarray(tbl)[None], wuv)


def _page_copies(pt_ref, cache, buf, sem, layer, npg, seq, slot, transposed):
    cps = []
    for p in range(npg):
        src = cache.at[layer, pt_ref[seq * npg + p]]
        if len(buf.shape) == 4:
            dst = buf.at[slot, p]
        elif transposed:
            pw = cache.shape[3]
            dst = buf.at[slot, :, pl.ds(p * pw, pw)]
        else:
            pw = cache.shape[2]
            dst = buf.at[slot, pl.ds(p * pw, pw)]
        cps.append(pltpu.make_async_copy(src, dst, sem.at[slot]))
    return cps


def _gather_step(pt_ref, caches, bufs, sems, transposed, layer, npg):
    b = pl.program_id(0)
    nb = pl.num_programs(0)

    def copies(seq, slot):
        out = []
        for cache, buf, sem, tr in zip(caches, bufs, sems, transposed):
            out += _page_copies(pt_ref, cache, buf, sem, layer, npg, seq, slot, tr)
        return out

    @pl.when(b == 0)
    def _():
        for cp in copies(0, 0):
            cp.start()

    @pl.when(b + 1 < nb)
    def _():
        for cp in copies(b + 1, lax.rem(b + 1, 2)):
            cp.start()

    slot = lax.rem(b, 2)
    for cp in copies(b, slot):
        cp.wait()
    return slot


def _paged_attn_kernel(pt_ref, *refs, layer, npg, main_t, has2, has_bias, has_mask):
    it = iter(refs)
    q_ref = next(it)
    q2_ref = next(it) if has2 else None
    bias_ref = next(it) if has_bias else None
    mask_ref = next(it) if has_mask else None
    xk_ref = next(it)
    xk2_ref = next(it) if has2 else None
    xb_ref = next(it)
    cache = next(it)
    cache2 = next(it) if has2 else None
    o_ref = next(it)
    buf = next(it)
    buf2 = next(it) if has2 else None
    sem = next(it)
    sem2 = next(it) if has2 else None

    caches, bufs, sems, trs = [cache], [buf], [sem], [main_t]
    if has2:
        caches.append(cache2)
        bufs.append(buf2)
        sems.append(sem2)
        trs.append(True)
    slot = _gather_step(pt_ref, caches, bufs, sems, trs, layer, npg)

    q = q_ref[0].astype(bf16)
    pages = buf[slot].astype(bf16)
    xk = xk_ref[0].astype(bf16)
    s = _dot(q, pages) if main_t else _dot_nt(q, pages)
    sx = _dot_nt(q, xk)
    if has2:
        q2 = q2_ref[0].astype(bf16)
        s = s + _dot(q2, buf2[slot].astype(bf16))
        sx = sx + _dot_nt(q2, xk2_ref[0].astype(bf16))
    if has_bias:
        s = s + bias_ref[...]
    if has_mask:
        reps = s.shape[0] // mask_ref.shape[1]
        s = s + jnp.concatenate([mask_ref[0]] * reps, axis=0)
    sx = sx + xb_ref[0]
    m = jnp.maximum(jnp.max(s, axis=-1, keepdims=True), jnp.max(sx, axis=-1, keepdims=True))
    pr = jnp.exp(s - m)
    px = jnp.exp(sx - m)
    l = jnp.sum(pr, axis=-1, keepdims=True) + jnp.sum(px, axis=-1, keepdims=True)
    pb = pr.astype(bf16)
    o = (_dot_nt(pb, pages) if main_t else _dot(pb, pages)) + _dot(px.astype(bf16), xk)
    o_ref[0] = jnp.where(m > 0.5 * NEG, o / l, 0.0)


def _paged_attn(page_table, cache, layer, q, xk, xb, main_t, q2=None, cache2=None, xk2=None, bias=None, mask=None):
    DB, R, DL = q.shape
    npg = page_table.shape[1]
    pw = cache.shape[3] if main_t else cache.shape[2]
    K = npg * pw
    has2 = q2 is not None
    seq = lambda b, pt: (b, 0, 0)
    in_specs = [pl.BlockSpec((1, R, DL), seq)]
    args = [q]
    if has2:
        d2 = q2.shape[-1]
        in_specs.append(pl.BlockSpec((1, R, d2), seq))
        args.append(q2)
    if bias is not None:
        in_specs.append(pl.BlockSpec((R, K), lambda b, pt: (0, 0)))
        args.append(bias)
    if mask is not None:
        in_specs.append(pl.BlockSpec((1, 8, K), seq))
        args.append(mask)
    in_specs.append(pl.BlockSpec((1, 8, DL), seq))
    args.append(xk)
    if has2:
        in_specs.append(pl.BlockSpec((1, 8, d2), seq))
        args.append(xk2)
    if xb.shape[0] == 1:
        in_specs.append(pl.BlockSpec((1, R, 8), lambda b, pt: (0, 0, 0)))
    else:
        in_specs.append(pl.BlockSpec((1, R, 8), seq))
    args.append(xb)
    in_specs.append(pl.BlockSpec(memory_space=pl.ANY))
    args.append(cache)
    scratch = [pltpu.VMEM((2, DL, K) if main_t else (2, K, DL), f32)]
    if has2:
        in_specs.append(pl.BlockSpec(memory_space=pl.ANY))
        args.append(cache2)
        scratch.append(pltpu.VMEM((2, d2, K), f32))
    scratch.append(pltpu.SemaphoreType.DMA((2,)))
    if has2:
        scratch.append(pltpu.SemaphoreType.DMA((2,)))
    grid_spec = pltpu.PrefetchScalarGridSpec(
        num_scalar_prefetch=1,
        grid=(DB,),
        in_specs=in_specs,
        out_specs=pl.BlockSpec((1, R, DL), seq),
        scratch_shapes=scratch,
    )
    return pl.pallas_call(
        functools.partial(_paged_attn_kernel, layer=layer, npg=npg, main_t=main_t, has2=has2,
                          has_bias=bias is not None, has_mask=mask is not None),
        out_shape=jax.ShapeDtypeStruct((DB, R, DL), f32),
        grid_spec=grid_spec,
        compiler_params=_cparams(("arbitrary",)),
        name="paged_attn",
    )(page_table.reshape(-1), *args)


def _compress_compute(load_rows, pe_ref, w1_ref, w2_ref, o_ref, nblk, W):
    half = nblk // 2
    acc = jnp.zeros((nblk, KV_LANES), f32)
    for r in range(CMP_BLOCK):
        x = jnp.concatenate([load_rows(r, half), load_rows(CMP_BLOCK + r, half)], axis=0) + pe_ref[r:r + 1, :]
        acc = acc + _dot(x.astype(bf16), w1_ref[r])
    hdn = acc * jax.nn.sigmoid(acc)
    out = _dot(hdn.astype(bf16), w2_ref[...])
    o_ref[0, 0:half, :] = out[:half]
    o_ref[0, W:W + half, :] = out[half:]
    if half < W:
        zeros = jnp.zeros((W - half, KV_LANES), f32)
        o_ref[0, half:W, :] = zeros
        o_ref[0, W + half:2 * W, :] = zeros


def _compress_kernel(xlo_ref, xhi_ref, pe_ref, w1_ref, w2_ref, o_ref, *, nblk, W):
    def load(r, n):
        rows = pl.ds(r, n, stride=2 * CMP_BLOCK)
        return jnp.concatenate([xlo_ref[0, rows, :], xhi_ref[0, rows, :]], axis=1)

    _compress_compute(load, pe_ref, w1_ref, w2_ref, o_ref, nblk, W)


def _compress_weight_specs(idx):
    return [
        pl.BlockSpec((CMP_BLOCK, KV_LANES), idx(2)),
        pl.BlockSpec((CMP_BLOCK, KV_LANES, KV_LANES), idx(3)),
        pl.BlockSpec((KV_LANES, KV_LANES), idx(2)),
    ]


def _compress(z3, colblk, pe_row, w1bd, w2p, W):
    B, T, _ = z3.shape
    nblk = T // CMP_BLOCK
    return pl.pallas_call(
        functools.partial(_compress_kernel, nblk=nblk, W=W),
        out_shape=jax.ShapeDtypeStruct((B, 2 * W, KV_LANES), f32),
        grid=(B,),
        in_specs=[pl.BlockSpec((1, T, LANES), lambda b: (b, 0, 2 * colblk)),
                  pl.BlockSpec((1, T, LANES), lambda b: (b, 0, 2 * colblk + 1))]
        + _compress_weight_specs(lambda n: (lambda b: (0,) * n)),
        out_specs=pl.BlockSpec((1, 2 * W, KV_LANES), lambda b: (b, 0, 0)),
        compiler_params=_cparams(("parallel",)),
        name="compress",
    )(z3, z3, pe_row, w1bd, w2p)


def _compress_paged_kernel(pt_ref, pe_ref, w1_ref, w2_ref, cache, o_ref, buf, xs, sem, *, layer, npg, W):
    slot = _gather_step(pt_ref, [cache], [buf], [sem], [True], layer, npg)

    def untranspose(p, carry):
        r0 = pl.multiple_of(p * PAGE, PAGE)
        for sg in range(KV_LANES // LANES):
            xs[sg, pl.ds(r0, PAGE), :] = buf[slot, p, sg * LANES:(sg + 1) * LANES, :].T
        return carry

    lax.fori_loop(0, npg, untranspose, 0)

    def load(r, n):
        rows = pl.ds(r, n, stride=2 * CMP_BLOCK)
        return jnp.concatenate([xs[sg, rows, :] for sg in range(KV_LANES // LANES)], axis=1)

    _compress_compute(load, pe_ref, w1_ref, w2_ref, o_ref, npg * PAGE // CMP_BLOCK, W)


def _compress_paged(page_table, cache_t, layer, pe_row, w1bd, w2p, W):
    DB, npg = page_table.shape
    grid_spec = pltpu.PrefetchScalarGridSpec(
        num_scalar_prefetch=1,
        grid=(DB,),
        in_specs=_compress_weight_specs(lambda n: (lambda b, pt: (0,) * n)) + [pl.BlockSpec(memory_space=pl.ANY)],
        out_specs=pl.BlockSpec((1, 2 * W, KV_LANES), lambda b, pt: (b, 0, 0)),
        scratch_shapes=[pltpu.VMEM((2, npg, KV_LANES, PAGE), f32),
                        pltpu.VMEM((KV_LANES // LANES, npg * PAGE, LANES), f32),
                        pltpu.SemaphoreType.DMA((2,))],
    )
    return pl.pallas_call(
        functools.partial(_compress_paged_kernel, layer=layer, npg=npg, W=W),
        out_shape=jax.ShapeDtypeStruct((DB, 2 * W, KV_LANES), f32),
        grid_spec=grid_spec,
        compiler_params=_cparams(("arbitrary",)),
        name="compress_paged",
    )(page_table.reshape(-1), pe_row, w1bd, w2p, cache_t)


def _masked_softmax(s):
    valid = s > 0.5 * NEG
    m = jnp.max(s, axis=-1, keepdims=True)
    e = jnp.where(valid, jnp.exp(s - m), 0.0)
    l = jnp.sum(e, axis=-1, keepdims=True)
    return e * jnp.where(l > 0.0, 1.0 / l, 0.0)


def _select_blocks(imp, q_pos, n_blk):
    R, W = imp.shape
    blk = lax.broadcasted_iota(jnp.int32, (R, W), 1)
    cur = q_pos // SEL_BLOCK
    forced = (blk == 0) | (blk == cur) | (blk == cur - 1)
    valid = (blk * SEL_BLOCK <= q_pos) & (blk < n_blk)
    score = jnp.where(valid, jnp.where(forced, FORCE, imp), -1.0)
    score = jnp.where(blk < n_blk, score, -jnp.inf)
    rank = jnp.zeros((R, W), f32)
    for c in range(n_blk):
        col = score[:, c:c + 1]
        beats = (col > score) | ((col == score) & (blk > c))
        rank = rank + jnp.where(beats, 1.0, 0.0)
    return jnp.where((rank < float(min(N_SEL, n_blk))) & valid, 1.0, 0.0)


def _nsa_cmp_prompt_kernel(q_ref, kv_ref, bias_ref, oc_ref, sel_ref, *, tq, W, n_blk):
    i = pl.program_id(2)
    kv = kv_ref[0]
    k = kv[:, :NSA_DH].astype(bf16)
    v = kv[:, NSA_DH:].astype(bf16)
    imp = jnp.zeros((tq, 2 * W), f32)
    for h in range(NSA_HPG):
        qh = (q_ref[0, :, h * NSA_DH:(h + 1) * NSA_DH] * NSA_SCALE).astype(bf16)
        p = _masked_softmax(_dot_nt(qh, k) + bias_ref[h])
        oc_ref[0, :, h * NSA_DH:(h + 1) * NSA_DH] = _dot(p.astype(bf16), v)
        imp = imp + p
    imp = imp[:, :W] + imp[:, W:]
    q_pos = i * tq + lax.broadcasted_iota(jnp.int32, (tq, 1), 0)
    sel_ref[0, 0] = _select_blocks(imp, q_pos, n_blk)


def _nsa_cmp_prompt(z3, kcv, bias, W, n_blk, tq=LANES):
    B, T, _ = z3.shape
    return pl.pallas_call(
        functools.partial(_nsa_cmp_prompt_kernel, tq=tq, W=W, n_blk=n_blk),
        out_shape=(jax.ShapeDtypeStruct((B, T, NSA_HEADS * NSA_DH), f32),
                   jax.ShapeDtypeStruct((B, NSA_G, T, W), f32)),
        grid=(B, NSA_G, T // tq),
        in_specs=[
            pl.BlockSpec((1, tq, NSA_HPG * NSA_DH), lambda b, g, i: (b, i, g)),
            pl.BlockSpec((1, 2 * W, 2 * NSA_DH), lambda b, g, i: (b, 0, g)),
            pl.BlockSpec((NSA_HPG, tq, 2 * W), lambda b, g, i: (g, i, 0)),
        ],
        out_specs=(pl.BlockSpec((1, tq, NSA_HPG * NSA_DH), lambda b, g, i: (b, i, g)),
                   pl.BlockSpec((1, 1, tq, W), lambda b, g, i: (b, g, i, 0))),
        compiler_params=_cparams(("parallel", "parallel", "parallel")),
        name="nsa_cmp_prompt",
    )(z3, kcv, bias)


def _nsa_cmp_sample_kernel(q_ref, kv_ref, bias_ref, e_ref, oc_ref, sel_ref, mexp_ref, imp_s, *,
                           bt, ts, W, n_blk, past):
    for bb in range(bt):
        for g in range(NSA_G):
            k = kv_ref[bb, :, g * 2 * NSA_DH:g * 2 * NSA_DH + NSA_DH].astype(bf16)
            v = kv_ref[bb, :, g * 2 * NSA_DH + NSA_DH:(g + 1) * 2 * NSA_DH].astype(bf16)
            qg = (q_ref[bb, g] * NSA_SCALE).astype(bf16)
            p = _masked_softmax(_dot_nt(qg, k) + bias_ref[g])
            oc_ref[bb, g] = _dot(p.astype(bf16), v)
            imp = jnp.sum(p.reshape(ts, NSA_HPG, 2 * W), axis=1)
            r0 = (bb * NSA_G + g) * ts
            imp_s[r0:r0 + ts, :] = imp[:, :W] + imp[:, W:]
    rows = bt * NSA_G * ts
    q_pos = past + lax.rem(lax.broadcasted_iota(jnp.int32, (rows, 1), 0), ts)
    sel = _select_blocks(imp_s[...], q_pos, n_blk)
    sel_ref[...] = sel.reshape(bt, NSA_G * ts, W)
    hit = _dot(sel.astype(bf16), e_ref[...])
    mexp_ref[...] = jnp.where(hit > 0.5, 0.0, NEG).reshape(bt, NSA_G * ts, mexp_ref.shape[2])


def _nsa_cmp_sample(q, kcv, bias, expand, W, n_blk, past, bt=8):
    DB, _, R, _ = q.shape
    ts = R // NSA_HPG
    K = expand.shape[1]
    rows = NSA_G * ts
    return pl.pallas_call(
        functools.partial(_nsa_cmp_sample_kernel, bt=bt, ts=ts, W=W, n_blk=n_blk, past=past),
        out_shape=(jax.ShapeDtypeStruct((DB, NSA_G, R, NSA_DH), f32),
                   jax.ShapeDtypeStruct((DB, rows, W), f32),
                   jax.ShapeDtypeStruct((DB, rows, K), f32)),
        grid=(DB // bt,),
        in_specs=[
            pl.BlockSpec((bt, NSA_G, R, NSA_DH), lambda b: (b, 0, 0, 0)),
            pl.BlockSpec((bt, 2 * W, KV_LANES), lambda b: (b, 0, 0)),
            pl.BlockSpec((NSA_G, R, 2 * W), lambda b: (0, 0, 0)),
            pl.BlockSpec((W, K), lambda b: (0, 0)),
        ],
        out_specs=(pl.BlockSpec((bt, NSA_G, R, NSA_DH), lambda b: (b, 0, 0, 0)),
                   pl.BlockSpec((bt, rows, W), lambda b: (b, 0, 0)),
                   pl.BlockSpec((bt, rows, K), lambda b: (b, 0, 0))),
        scratch_shapes=[pltpu.VMEM((bt * rows, W), f32)],
        compiler_params=_cparams(("parallel",)),
        name="nsa_cmp_sample",
    )(q, kcv, bias, expand)


def _stack_heads(q_ref, q_s, tq):
    q_s[:, 0:2 * NSA_DH] = jnp.zeros((q_s.shape[0], 2 * NSA_DH), bf16)
    for h in range(NSA_HPG):
        q_s[h * tq:(h + 1) * tq, 0:NSA_DH] = (q_ref[0, :, h * NSA_DH:(h + 1) * NSA_DH] * NSA_SCALE).astype(bf16)


def _unstack_heads(o, o_ref, tq):
    for h in range(NSA_HPG):
        o_ref[0, :, h * NSA_DH:(h + 1) * NSA_DH] = o[h * tq:(h + 1) * tq, NSA_DH:]


def _nsa_sel_kernel(ii_ref, jj_ref, fl_ref, q_ref, kv_ref, tbl_ref, sel_ref, et_ref, o_ref,
                    q_s, m_s, l_s, acc_s, *, tq, tk):
    p = pl.program_id(2)
    i = ii_ref[p]
    j = jj_ref[p]
    fl = fl_ref[p]

    @pl.when((fl & 1) == 1)
    def _():
        _stack_heads(q_ref, q_s, tq)
        selneg = jnp.where(sel_ref[0, 0] > 0.5, 0.0, NEG).astype(bf16)
        for h in range(NSA_HPG):
            q_s[h * tq:(h + 1) * tq, 2 * NSA_DH:] = selneg
        _flash_init(m_s, l_s, acc_s)

    kvb = kv_ref[0].astype(bf16)
    s = _dot_nt(q_s[...], jnp.concatenate([kvb, et_ref[...]], axis=1))
    s = _add_bias_tiles(s, tbl_ref, i, j * (tk // LANES), tbl_ref.shape[1] - 1)
    _flash_update(s, kvb, m_s, l_s, acc_s)

    @pl.when((fl & 2) == 2)
    def _():
        _unstack_heads(acc_s[...] / l_s[...], o_ref, tq)


def _nsa_sel(z3, kvcol, tbl, sel, expand_t, tk=FLASH_TK):
    B, T, _ = z3.shape
    tq = LANES
    tk = min(tk, T)
    W = sel.shape[-1]
    R = NSA_HPG * tq
    ii, jj, fl = _pairs(T // tq, lambda i: 0, lambda i: (i * tq + tq - 1) // tk)
    npairs = int(ii.shape[0])
    kvb = tk // LANES
    grid_spec = pltpu.PrefetchScalarGridSpec(
        num_scalar_prefetch=3,
        grid=(B, NSA_G, npairs),
        in_specs=[
            pl.BlockSpec((1, tq, NSA_HPG * NSA_DH), lambda b, g, p, ii, jj, fl: (b, ii[p], g)),
            pl.BlockSpec((1, tk, 2 * NSA_DH), lambda b, g, p, ii, jj, fl: (b, jj[p], kvcol + g)),
            pl.BlockSpec((1, tbl.shape[1], R, LANES), lambda b, g, p, ii, jj, fl: (g, 0, 0, 0)),
            pl.BlockSpec((1, 1, tq, W), lambda b, g, p, ii, jj, fl: (b, g, ii[p], 0)),
            pl.BlockSpec((tk, W), lambda b, g, p, ii, jj, fl: (jj[p], 0)),
        ],
        out_specs=pl.BlockSpec((1, tq, NSA_HPG * NSA_DH), lambda b, g, p, ii, jj, fl: (b, ii[p], g)),
        scratch_shapes=[pltpu.VMEM((R, 2 * NSA_DH + W), bf16), pltpu.VMEM((R, 1), f32), pltpu.VMEM((R, 1), f32),
                        pltpu.VMEM((R, 2 * NSA_DH), f32)],
    )
    del kvb
    return pl.pallas_call(
        functools.partial(_nsa_sel_kernel, tq=tq, tk=tk),
        out_shape=jax.ShapeDtypeStruct((B, T, NSA_HEADS * NSA_DH), f32),
        grid_spec=grid_spec,
        compiler_params=_cparams(("parallel", "parallel", "arbitrary")),
        name="nsa_sel",
    )(ii, jj, fl, z3, z3, tbl, sel, expand_t)


def _nsa_win_kernel(q_ref, kv_ref, tbl_ref, o_ref, q_s, *, tq, nsub):
    i = pl.program_id(2)
    _stack_heads(q_ref, q_s, tq)
    tile0 = jnp.maximum(i - (nsub - 1), 0)
    kvb = kv_ref[0, pl.ds(pl.multiple_of(tile0 * LANES, LANES), nsub * LANES), :].astype(bf16)
    s = _add_bias_tiles(_dot_nt(q_s[...], kvb), tbl_ref, i, tile0, tbl_ref.shape[1] - 1)
    m = jnp.max(s, axis=-1, keepdims=True)
    pr = jnp.exp(s - m)
    l = jnp.sum(pr, axis=-1, keepdims=True)
    _unstack_heads(_dot(pr.astype(bf16), kvb) / l, o_ref, tq)


def _nsa_win(z3, kvcol, tbl):
    B, T, _ = z3.shape
    tq = LANES
    nsub = tbl.shape[1] - 1
    assert T >= nsub * LANES
    R = NSA_HPG * tq
    return pl.pallas_call(
        functools.partial(_nsa_win_kernel, tq=tq, nsub=nsub),
        out_shape=jax.ShapeDtypeStruct((B, T, NSA_HEADS * NSA_DH), f32),
        grid=(B, NSA_G, T // tq),
        in_specs=[
            pl.BlockSpec((1, tq, NSA_HPG * NSA_DH), lambda b, g, i: (b, i, g)),
            pl.BlockSpec((1, T, 2 * NSA_DH), lambda b, g, i: (b, 0, kvcol + g)),
            pl.BlockSpec((1, nsub + 1, R, LANES), lambda b, g, i: (g, 0, 0, 0)),
        ],
        out_specs=pl.BlockSpec((1, tq, NSA_HPG * NSA_DH), lambda b, g, i: (b, i, g)),
        scratch_shapes=[pltpu.VMEM((R, 2 * NSA_DH), bf16)],
        compiler_params=_cparams(("parallel", "parallel", "parallel")),
        name="nsa_win",
    )(z3, z3, tbl)


def _bucket_np(n):
    exact = N_BUCKETS // 2
    nf = np.maximum(n, 1).astype(np.float32)
    large = exact + (np.log(nf / exact) / math.log(MAX_DISTANCE / exact) * (N_BUCKETS - exact)).astype(np.int32)
    return np.where(n < exact, n, np.minimum(large, N_BUCKETS - 1)).astype(np.int32)


def _bucket_starts():
    b = _bucket_np(np.arange(4 * MAX_DISTANCE))
    return [int(np.argmax(b >= k)) for k in range(N_BUCKETS)]


def _t5_bias(rel_bias, dist, mask):
    n = jnp.maximum(dist, 0)[None]
    tbl = rel_bias.astype(f32).T
    col = lambda k: tbl[:, k].reshape((NSA_HEADS,) + (1,) * dist.ndim)
    out = jnp.broadcast_to(col(0), (NSA_HEADS,) + dist.shape)
    for k, start in enumerate(_bucket_starts()):
        if k:
            out = jnp.where(n >= start, col(k), out)
    return jnp.where(mask[None], out, NEG)


def _cmp_block_of_lane(W, n_cmp):
    lane = np.arange(2 * W)
    half = n_cmp // 2
    blk = np.where(lane < W, 2 * lane, 2 * (lane - W) + 1)
    ok = np.where(lane < W, lane < half, (lane - W) < half)
    return np.where(ok, blk, -1)


def _rope_tables(pos, half):
    inv = ROPE_BASE ** (-jnp.arange(half, dtype=f32) / half)
    ang = pos.astype(f32)[:, None] * inv[None, :]
    return jnp.cos(ang)[:, None, :], jnp.sin(ang)[:, None, :]


def _rope(x, cs):
    cos, sin = cs
    half = x.shape[-1] // 2
    x1, x2 = x[..., :half], x[..., half:]
    return jnp.concatenate([x1 * cos - x2 * sin, x2 * cos + x1 * sin], axis=-1)


def _block_diag(blocks):
    n = len(blocks)
    r, c = blocks[0].shape
    out = jnp.zeros((n * r, n * c), blocks[0].dtype)
    for i, blk in enumerate(blocks):
        out = out.at[i * r:(i + 1) * r, i * c:(i + 1) * c].set(blk)
    return out


def _pages_t(cache):
    L, pool, rows = cache.shape[:3]
    nd = cache.ndim
    return cache.transpose((0, 1) + tuple(range(3, nd)) + (2,)).reshape(L, pool, -1, rows)


def kernel(x_prompt, x_sample, cache_mla_ckv, cache_mla_krope, cache_nsa_cmp, cache_nsa_sel, page_table, state_nsa_win, state_ret, state_ffn_conv, rel_bias, norm_mix, norm_ffn, final_norm, ev_w_in, mla_q_norm, mla_w_qb, mla_kv_norm, mla_w_uk, mla_w_uv, ev_w_out, od_w_in, cmp_pe, cmp_w1, cmp_w2, od_w_out, ffn_w_up, ffn_conv_w, ffn_conv_b, ffn_w_down):
    B, T, D = x_prompt.shape
    DB, TS, _ = x_sample.shape
    depth = norm_mix.shape[0]
    npg = page_table.shape[1]
    past = npg * PAGE
    Mp, Ms = B * T, DB * TS
    assert T % LANES == 0 and NSA_G * TS == 8 and past % (2 * CMP_BLOCK) == 0 and TS < CMP_BLOCK
    assert past % SEL_BLOCK + TS <= SEL_BLOCK

    pos_p = jnp.arange(T)
    pos_s = past + jnp.arange(TS)
    pos_all = jnp.concatenate([jnp.tile(pos_p, B), jnp.tile(pos_s, DB)])
    rope16 = _rope_tables(pos_all, MLA_ROPE // 2)
    rope64 = _rope_tables(pos_all, RET_D // 2)

    h = jnp.concatenate([x_prompt.reshape(Mp, D), x_sample.reshape(Ms, D)], axis=0)

    tq = LANES
    n_blk_p = T // SEL_BLOCK
    Wp = 128
    assert n_blk_p <= Wp
    lane_blk = _cmp_block_of_lane(Wp, T // CMP_BLOCK)
    end = jnp.asarray(lane_blk * CMP_BLOCK + (CMP_BLOCK - 1))
    dist = jnp.arange(T)[:, None] - end[None, :]
    bias_cmp_p = _t5_bias(rel_bias, dist, (dist >= 0) & jnp.asarray(lane_blk >= 0)[None, :])

    def flash_table(nd, window):
        r = jnp.arange(tq)[None, :, None]
        c = jnp.arange(LANES)[None, None, :]
        dd = jnp.arange(nd + 1)[:, None, None] * LANES + r - c
        mask = (dd >= 0) & (jnp.arange(nd + 1)[:, None, None] < nd)
        if window:
            mask &= dd < WINDOW
        t = _t5_bias(rel_bias, dd, mask)
        t = t.reshape(NSA_G, NSA_HPG, nd + 1, tq, LANES).transpose(0, 2, 1, 3, 4)
        return t.reshape(NSA_G, nd + 1, NSA_HPG * tq, LANES)

    tbl_sel = flash_table(3, False)
    tbl_win = flash_table(WINDOW // LANES + 1, True)
    expand_pt = (jnp.arange(T)[:, None] // SEL_BLOCK == jnp.arange(Wp)[None, :]).astype(bf16)

    n_blk_s = -(-(past + TS) // SEL_BLOCK)
    Ws = max(128, -(-n_blk_s // 128) * 128)
    n_cmp_s = past // CMP_BLOCK
    assert n_cmp_s // 2 <= Ws
    lane_blk_s = _cmp_block_of_lane(Ws, n_cmp_s)
    end_s = jnp.asarray(lane_blk_s * CMP_BLOCK + (CMP_BLOCK - 1))
    qpos_s = past + jnp.arange(TS)
    dist_s = qpos_s[:, None] - end_s[None, :]
    bias_cmp_s = _t5_bias(rel_bias, dist_s, (dist_s >= 0) & jnp.asarray(lane_blk_s >= 0)[None, :])
    bias_cmp_s = bias_cmp_s.reshape(NSA_G, NSA_HPG, TS, 2 * Ws).transpose(0, 2, 1, 3).reshape(NSA_G, TS * NSA_HPG, 2 * Ws)
    expand_s = (jnp.arange(Ws)[:, None] == jnp.arange(past)[None, :] // SEL_BLOCK).astype(bf16)

    def sample_rows_table(kpos, window):
        dd = qpos_s[:, None] - kpos[None, :]
        mask = dd >= 0
        if window:
            mask &= (dd < WINDOW) & (kpos >= 0)[None, :]
        t = _t5_bias(rel_bias, dd, mask)
        t = t.reshape(NSA_G, NSA_HPG, TS, -1).transpose(1, 0, 2, 3)
        return t.reshape(NSA_HPG * NSA_G * TS, -1)

    bias_sel_s = sample_rows_table(jnp.arange(past), False)
    wb = state_nsa_win.shape[2]
    bias_win_s = sample_rows_table(past - wb + jnp.arange(wb), True)
    new_pos = jnp.concatenate([qpos_s, jnp.full((8 - TS,), 10 ** 9, qpos_s.dtype)])
    xb_nsa = sample_rows_table(new_pos, False)
    win_pages = jnp.arange(DB, dtype=jnp.int32).reshape(DB, 1)

    tt = np.repeat(np.arange(TS), MLA_HEADS)
    xb_mla = jnp.asarray(np.where(np.arange(8)[None, :] <= tt[:, None], 0.0, NEG).astype(np.float32))[None]

    krope_t = _pages_t(cache_mla_krope)
    cmp_t = _pages_t(cache_nsa_cmp)
    sel_t = _pages_t(cache_nsa_sel)
    win_t = _pages_t(state_nsa_win)

    outs = {k: [] for k in ("ckv_p", "ckv_s", "kr_p", "kr_s", "cmp_p", "cmp_s", "sel_p", "sel_s",
                            "win_p", "win_s", "ret_p", "ret_s", "conv_p", "conv_s")}

    def split(a):
        return a[:Mp], a[Mp:]

    def pad_rows(a, n):
        return jnp.pad(a, ((0, 0), (0, n - a.shape[1]), (0, 0)))

    for layer in range(depth):
        j = layer // 2
        if layer % 2 == 0:
            w = ev_w_in[j]
            a0 = MLA_Q_RANK + MLA_KV_RANK
            c0 = a0 + MLA_ROPE
            w_p = jnp.concatenate([w[:, :a0], w[:, c0:], w[:, a0:c0], jnp.zeros((D, 96), f32)], axis=1).astype(bf16)
            z = _mm(h, w_p, g=norm_mix[layer])
            q = _mm(z, mla_w_qb[j].astype(bf16), g=mla_q_norm[j])
            zc = z[:, MLA_Q_RANK:a0]
            ckv = zc * lax.rsqrt(jnp.mean(zc * zc, axis=-1, keepdims=True) + EPS) * mla_kv_norm[j]
            r0 = a0
            nr = RET_HEADS * RET_D
            kr = _rope(z[:, r0 + 4 * nr:r0 + 4 * nr + MLA_ROPE][:, None, :], rope16)[:, 0]
            q3 = q.reshape(-1, MLA_HEADS, MLA_NOPE + MLA_ROPE)
            qn = q3[..., :MLA_NOPE].reshape(-1, MLA_HEADS * MLA_NOPE)
            qp = (_rope(q3[..., MLA_NOPE:], rope16) * MLA_SCALE).reshape(-1, MLA_HEADS * MLA_ROPE)
            rq = _rope(z[:, r0:r0 + nr].reshape(-1, RET_HEADS, RET_D), rope64).reshape(-1, nr)
            rk = (_rope(z[:, r0 + nr:r0 + 2 * nr].reshape(-1, RET_HEADS, RET_D), rope64) * RET_D ** -0.5).reshape(-1, nr)
            rv = z[:, r0 + 2 * nr:r0 + 3 * nr]
            rg = z[:, r0 + 3 * nr:r0 + 4 * nr]

            wuk_t = mla_w_uk[j].transpose(1, 2, 0).astype(bf16)
            wuv = mla_w_uv[j].transpose(1, 0, 2).astype(bf16)
            qn_p, qn_s = split(qn)
            qp_p, qp_s = split(qp)
            ckv_p, ckv_s = split(ckv)
            kr_p, kr_s = split(kr)

            o_mla_p = _mla_prompt(qn_p.reshape(B, T, -1), qp_p.reshape(B, T, -1), ckv_p.reshape(B, T, -1),
                                  kr_p.reshape(B, T, -1), wuk_t, wuv).reshape(Mp, -1)
            wuk_bd = _block_diag([wuk_t[hh] for hh in range(MLA_HEADS)])
            qa_s = (_mm(qn_s, wuk_bd) * MLA_SCALE).reshape(DB, TS * MLA_HEADS, MLA_KV_RANK)
            o_lat = _paged_attn(page_table, cache_mla_ckv, j, qa_s,
                                pad_rows(ckv_s.reshape(DB, TS, -1), 8), xb_mla, False,
                                q2=qp_s.reshape(DB, TS * MLA_HEADS, MLA_ROPE), cache2=krope_t,
                                xk2=pad_rows(kr_s.reshape(DB, TS, -1), 8))
            wuv_bd = _block_diag([wuv[hh] for hh in range(MLA_HEADS)])
            o_mla_s = _mm(o_lat.reshape(Ms, MLA_HEADS * MLA_KV_RANK), wuv_bd)

            rq_p, rq_s = split(rq)
            rk_p, rk_s = split(rk)
            rv_p, rv_s = split(rv)
            ro_p, rs_p = _retention(rq_p.reshape(B, T, nr), rk_p.reshape(B, T, nr), rv_p.reshape(B, T, nr),
                                    jnp.zeros((B, RET_HEADS, RET_D, RET_D), f32), float(min(RET_D, T)))
            ro_s, rs_s = _retention(pad_rows(rq_s.reshape(DB, TS, nr), RET_D), pad_rows(rk_s.reshape(DB, TS, nr), RET_D),
                                    pad_rows(rv_s.reshape(DB, TS, nr), RET_D), state_ret[j], float(TS))
            ro = jnp.concatenate([ro_p.reshape(Mp, nr), ro_s[:, :TS].reshape(Ms, nr)], axis=0)
            ro = ro.reshape(-1, RET_HEADS, RET_D)
            rn = ro * lax.rsqrt(jnp.mean(ro * ro, axis=-1, keepdims=True) + EPS)
            o_ret = jax.nn.silu(rg) * rn.reshape(-1, nr)
            mix = jnp.concatenate([jnp.concatenate([o_mla_p, o_mla_s], axis=0), o_ret], axis=1)
            h = _mm(mix, ev_w_out[j].astype(bf16), resid=h)

            outs["ckv_p"].append(ckv_p.reshape(B, T, -1))
            outs["ckv_s"].append(ckv_s.reshape(DB, TS, -1))
            outs["kr_p"].append(kr_p.reshape(B, T, -1))
            outs["kr_s"].append(kr_s.reshape(DB, TS, -1))
            outs["ret_p"].append(rs_p)
            outs["ret_s"].append(rs_s)
        else:
            w = od_w_in[j]
            nq_cols = NSA_HEADS * NSA_DH
            wkv = w[:, nq_cols:nq_cols + 3 * KV_LANES].reshape(D, 3, 2, NSA_G, NSA_DH)
            wsel = wkv[:, 1].transpose(0, 2, 1, 3).reshape(D, KV_LANES)
            wwin = wkv[:, 2].transpose(0, 2, 1, 3).reshape(D, KV_LANES)
            ngate = 3 * NSA_HEADS
            w_p = jnp.concatenate([w[:, :nq_cols + KV_LANES], wsel, wwin, w[:, nq_cols + 3 * KV_LANES:],
                                   jnp.zeros((D, 128 - ngate), f32)], axis=1).astype(bf16)
            z = _mm(h, w_p, g=norm_mix[layer])
            NZ = z.shape[1]
            c_cmp = nq_cols
            c_sel = c_cmp + KV_LANES
            c_win = c_sel + KV_LANES
            c_gate = c_win + KV_LANES
            z_p, z_s = split(z)
            z3 = z_p.reshape(B, T, NZ)
            gates = jax.nn.sigmoid(z[:, c_gate:c_gate + ngate]).reshape(-1, NSA_HEADS, 3)

            def orig_kv(a):
                return a.reshape(a.shape[0], NSA_G, 2, NSA_DH).transpose(0, 2, 1, 3)

            cmp_rows = z[:, c_cmp:c_sel].reshape(-1, 2, NSA_G, NSA_DH)
            sel_rows = orig_kv(z[:, c_sel:c_win])
            win_rows = orig_kv(z[:, c_win:c_gate])

            w1k = cmp_w1[j, 0].reshape(CMP_BLOCK, NSA_DH, -1)
            w1v = cmp_w1[j, 1].reshape(CMP_BLOCK, NSA_DH, -1)
            w1bd = jnp.stack([_block_diag([w1k[r], w1k[r], w1v[r], w1v[r]]) for r in range(CMP_BLOCK)]).astype(bf16)
            w2k, w2v = cmp_w2[j, 0], cmp_w2[j, 1]
            zb = jnp.zeros_like(w2k)
            w2p = jnp.concatenate([
                jnp.concatenate([w2k, zb, zb, zb], axis=1),
                jnp.concatenate([zb, zb, w2k, zb], axis=1),
                jnp.concatenate([zb, w2v, zb, zb], axis=1),
                jnp.concatenate([zb, zb, zb, w2v], axis=1)], axis=0).astype(bf16)
            pe_row = jnp.concatenate([cmp_pe[j, 0], cmp_pe[j, 0], cmp_pe[j, 1], cmp_pe[j, 1]], axis=1)

            kcv_p = _compress(z3, c_cmp // KV_LANES, pe_row, w1bd, w2p, Wp)
            oc_p, sel_p = _nsa_cmp_prompt(z3, kcv_p, bias_cmp_p, Wp, n_blk_p)
            os_p = _nsa_sel(z3, c_sel // LANES, tbl_sel, sel_p, expand_pt)
            ow_p = _nsa_win(z3, c_win // LANES, tbl_win)

            kcv_s = _compress_paged(page_table, cmp_t, j, pe_row, w1bd, w2p, Ws)
            q_s5 = z_s[:, :nq_cols].reshape(DB, TS, NSA_G, NSA_HPG, NSA_DH)
            q_tj = q_s5.transpose(0, 2, 1, 3, 4).reshape(DB, NSA_G, TS * NSA_HPG, NSA_DH)
            oc_s, sel_s, mexp_s = _nsa_cmp_sample(q_tj, kcv_s, bias_cmp_s, expand_s, Ws, n_blk_s, past)
            oc_s = oc_s.reshape(DB, NSA_G, TS, NSA_HPG, NSA_DH).transpose(0, 2, 1, 3, 4)
            q_jgt = q_s5.transpose(0, 3, 2, 1, 4) * NSA_SCALE
            lane_g = (jnp.arange(KV_LANES) // NSA_DH)[None, None, None, None, :] == jnp.arange(NSA_G)[None, None, :, None, None]
            q_aug = jnp.where(lane_g, jnp.tile(q_jgt, (1, 1, 1, 1, KV_LANES // NSA_DH)), 0.0)
            q_aug = q_aug.reshape(DB, NSA_HPG * NSA_G * TS, KV_LANES)
            new_sel = sel_rows[Mp:].reshape(DB, TS, KV_LANES)
            new_win = win_rows[Mp:].reshape(DB, TS, KV_LANES)
            cur_blk = past // SEL_BLOCK
            member = sel_s[:, :, cur_blk].reshape(DB, 1, NSA_G * TS)
            member = jnp.broadcast_to(member, (DB, NSA_HPG, NSA_G * TS)).reshape(DB, -1, 1)
            xb_sel = jnp.where(member > 0.5, xb_nsa[None], NEG)
            o_sel = _paged_attn(page_table, sel_t, j, q_aug, pad_rows(new_sel, 8), xb_sel, True,
                                bias=bias_sel_s, mask=mexp_s)
            o_win = _paged_attn(win_pages, win_t, j, q_aug, pad_rows(new_win, 8), xb_nsa[None], True,
                                bias=bias_win_s)

            def unpack(o):
                o = o.reshape(DB, NSA_HPG, NSA_G, TS, 2, NSA_G, NSA_DH)[:, :, :, :, 1]
                o = jnp.stack([o[:, :, g, :, g] for g in range(NSA_G)], axis=2)
                return o.transpose(0, 3, 2, 1, 4)

            os_s, ow_s = unpack(o_sel), unpack(o_win)

            def cat(p_, s_):
                return jnp.concatenate([p_.reshape(Mp, NSA_HEADS, NSA_DH), s_.reshape(Ms, NSA_HEADS, NSA_DH)], axis=0)

            o = (gates[..., 0:1] * cat(oc_p, oc_s) + gates[..., 1:2] * cat(os_p, os_s)
                 + gates[..., 2:3] * cat(ow_p, ow_s)).reshape(-1, nq_cols)
            h = _mm(o, od_w_out[j].astype(bf16), resid=h)

            wrows = min(WINDOW, T)
            outs["cmp_p"].append(cmp_rows[:Mp].reshape(B, T, 2, NSA_G, NSA_DH))
            outs["cmp_s"].append(cmp_rows[Mp:].reshape(DB, TS, 2, NSA_G, NSA_DH))
            outs["sel_p"].append(sel_rows[:Mp].reshape(B, T, 2, NSA_G, NSA_DH))
            outs["sel_s"].append(sel_rows[Mp:].reshape(DB, TS, 2, NSA_G, NSA_DH))
            outs["win_p"].append(win_rows[:Mp].reshape(B, T, 2, NSA_G, NSA_DH)[:, T - wrows:])
            win_cat = jnp.concatenate([state_nsa_win[j], win_rows[Mp:].reshape(DB, TS, 2, NSA_G, NSA_DH)], axis=1)
            outs["win_s"].append(win_cat[:, TS:])

        up = _mm(h, ffn_w_up[layer].astype(bf16), g=norm_ffn[layer])
        w_down = ffn_w_down[layer].astype(bf16)
        a_s = up[Mp:, :D_FF].reshape(DB, TS, D_FF)
        a_ext_s = jnp.concatenate([state_ffn_conv[layer], a_s], axis=1)
        h_p = _ffn_down_prompt(up, ffn_conv_w[layer], ffn_conv_b[layer], w_down, h, Mp, T)
        h_s = _ffn_down_rows(up, a_ext_s[:, 1:1 + TS].reshape(Ms, D_FF), a_ext_s[:, 0:TS].reshape(Ms, D_FF),
                             ffn_conv_w[layer], ffn_conv_b[layer], w_down, h, Mp)
        h = jnp.concatenate([h_p, h_s], axis=0)
        outs["conv_p"].append(up[:Mp, :D_FF].reshape(B, T, D_FF)[:, T - 2:])
        outs["conv_s"].append(a_ext_s[:, TS:])

    y = _rmsnorm(h, final_norm)
    st = lambda k: jnp.stack(outs[k])
    return (y[:Mp].reshape(B, T, D), y[Mp:].reshape(DB, TS, D),
            st("ckv_p"), st("ckv_s"), st("kr_p"), st("kr_s"),
            st("cmp_p"), st("cmp_s"), st("sel_p"), st("sel_s"),
            st("win_p"), st("win_s"), st("ret_p"), st("ret_s"),
            st("conv_p"), st("conv_s"))
```

```python
import functools
import math

import numpy as np
import jax
import jax.numpy as jnp
from jax import lax
from jax.experimental import pallas as pl
from jax.experimental.pallas import tpu as pltpu

f32 = jnp.float32
bf16 = jnp.bfloat16

PAGE = 128
MLA_HEADS = 8
MLA_Q_RANK = 384
MLA_KV_RANK = 256
MLA_NOPE = 64
MLA_ROPE = 32
MLA_V = 64
MLA_SCALE = (MLA_NOPE + MLA_ROPE) ** -0.5
RET_HEADS = 4
RET_D = 128
NSA_HEADS = 16
NSA_G = 2
NSA_HPG = 8
NSA_DH = 64
NSA_SCALE = NSA_DH ** -0.5
CMP_BLOCK = 32
SEL_BLOCK = 64
N_SEL = 16
WINDOW = 512
D_FF = 2816
N_BUCKETS = 32
MAX_DISTANCE = 128
ROPE_BASE = 10000.0
EPS = 1e-6
NEG = -1e30
FORCE = 1e4
KV_LANES = 2 * NSA_G * NSA_DH
LANES = 128
FLASH_TK = 1024
XS_PITCH = 72

VMEM_LIMIT = 56 * 1024 * 1024


def _cparams(sem):
    return pltpu.CompilerParams(dimension_semantics=sem, vmem_limit_bytes=VMEM_LIMIT)


def _dot(a, b):
    return jnp.dot(a, b, preferred_element_type=f32)


def _dot_nt(a, b):
    return lax.dot_general(a, b, (((1,), (1,)), ((), ())), preferred_element_type=f32)


def _row_tile(m, cap):
    for t in (512, 256, 128, 64, 32, 16, 8):
        if t <= cap and m % t == 0:
            return t
    return m


def _pick_tile(n, cap):
    best = None
    for t in range(128, min(n, cap) + 1, 128):
        if n % t == 0:
            best = t
    return best if best is not None else n


def _mm_kernel(*refs, norm, resid):
    it = iter(refs)
    x_ref = next(it)
    g_ref = next(it) if norm else None
    w_ref = next(it)
    r_ref = next(it) if resid else None
    o_ref = next(it)
    xn_ref = next(it)

    @pl.when(pl.program_id(1) == 0)
    def _():
        x = x_ref[...]
        if norm:
            x = x * lax.rsqrt(jnp.mean(x * x, axis=-1, keepdims=True) + EPS) * g_ref[...]
        xn_ref[...] = x.astype(bf16)

    acc = _dot(xn_ref[...], w_ref[...])
    if resid:
        acc = acc + r_ref[...]
    o_ref[...] = acc


def _mm(x, w, g=None, resid=None, tm=512):
    M = x.shape[0]
    K, N = w.shape
    tm = _row_tile(M, tm)
    assert K == x.shape[1] or K % 128 == 0
    tn = _pick_tile(N, 1536)
    norm = g is not None
    has_r = resid is not None
    in_specs = [pl.BlockSpec((tm, K), lambda i, j: (i, 0))]
    args = [x]
    if norm:
        in_specs.append(pl.BlockSpec((1, K), lambda i, j: (0, 0)))
        args.append(g.reshape(1, K).astype(f32))
    in_specs.append(pl.BlockSpec((K, tn), lambda i, j: (0, j)))
    args.append(w)
    if has_r:
        in_specs.append(pl.BlockSpec((tm, tn), lambda i, j: (i, j)))
        args.append(resid)
    return pl.pallas_call(
        functools.partial(_mm_kernel, norm=norm, resid=has_r),
        out_shape=jax.ShapeDtypeStruct((M, N), f32),
        grid=(M // tm, N // tn),
        in_specs=in_specs,
        out_specs=pl.BlockSpec((tm, tn), lambda i, j: (i, j)),
        scratch_shapes=[pltpu.VMEM((tm, K), bf16)],
        compiler_params=_cparams(("parallel", "arbitrary")),
        name="mm",
    )(*args)


def _rmsnorm_kernel(x_ref, g_ref, o_ref):
    x = x_ref[...]
    o_ref[...] = x * lax.rsqrt(jnp.mean(x * x, axis=-1, keepdims=True) + EPS) * g_ref[...]


def _rmsnorm(x, g, tm=512):
    M, D = x.shape
    tm = _row_tile(M, tm)
    return pl.pallas_call(
        _rmsnorm_kernel,
        out_shape=jax.ShapeDtypeStruct((M, D), f32),
        grid=(M // tm,),
        in_specs=[pl.BlockSpec((tm, D), lambda i: (i, 0)), pl.BlockSpec((1, D), lambda i: (0, 0))],
        out_specs=pl.BlockSpec((tm, D), lambda i: (i, 0)),
        compiler_params=_cparams(("parallel",)),
        name="rmsnorm",
    )(x, g.reshape(1, D))


def _ffn_gate_down(a, a1, a2, u, cw_ref, cb_ref, w_ref, r_ref, o_ref):
    cw = cw_ref[...]
    c = cb_ref[...] + a2 * cw[0:1] + a1 * cw[1:2] + a * cw[2:3]
    gate = c * jax.nn.sigmoid(c) * u
    o_ref[...] = _dot(gate.astype(bf16), w_ref[...]) + r_ref[...]


def _ffn_down_prompt_kernel(a_ref, prev_ref, u_ref, cw_ref, cb_ref, w_ref, r_ref, o_ref, *, tiles_per_seq):
    a = a_ref[...]
    keep = jnp.where(lax.rem(pl.program_id(0), tiles_per_seq) == 0, 0.0, 1.0)
    p7 = prev_ref[7:8, :] * keep
    p6 = prev_ref[6:7, :] * keep
    row = lax.broadcasted_iota(jnp.int32, a.shape, 0)
    a1 = jnp.where(row == 0, p7, pltpu.roll(a, 1, 0))
    a2 = jnp.where(row == 0, p6, jnp.where(row == 1, p7, pltpu.roll(a, 2, 0)))
    _ffn_gate_down(a, a1, a2, u_ref[...], cw_ref, cb_ref, w_ref, r_ref, o_ref)


def _ffn_down_prompt(up, conv_w, conv_b, w_down, resid, Mp, T, tm=256):
    D = w_down.shape[1]
    tm = _row_tile(T, tm)
    row = lambda i: (i, 0)
    return pl.pallas_call(
        functools.partial(_ffn_down_prompt_kernel, tiles_per_seq=T // tm),
        out_shape=jax.ShapeDtypeStruct((Mp, D), f32),
        grid=(Mp // tm,),
        in_specs=[
            pl.BlockSpec((tm, D_FF), row),
            pl.BlockSpec((8, D_FF), lambda i: (jnp.maximum(i * (tm // 8) - 1, 0), 0)),
            pl.BlockSpec((tm, D_FF), lambda i: (i, 1)),
            pl.BlockSpec((3, D_FF), lambda i: (0, 0)),
            pl.BlockSpec((1, D_FF), lambda i: (0, 0)),
            pl.BlockSpec((D_FF, D), lambda i: (0, 0)),
            pl.BlockSpec((tm, D), row),
        ],
        out_specs=pl.BlockSpec((tm, D), row),
        compiler_params=_cparams(("parallel",)),
        name="ffn_down_prompt",
    )(up, up, up, conv_w, conv_b.reshape(1, D_FF), w_down, resid)


def _ffn_down_rows_kernel(a_ref, a1_ref, a2_ref, u_ref, cw_ref, cb_ref, w_ref, r_ref, o_ref):
    _ffn_gate_down(a_ref[...], a1_ref[...], a2_ref[...], u_ref[...], cw_ref, cb_ref, w_ref, r_ref, o_ref)


def _ffn_down_rows(up, a1, a2, conv_w, conv_b, w_down, resid, row0, tm=256):
    Ms = a1.shape[0]
    D = w_down.shape[1]
    tm = _row_tile(math.gcd(Ms, row0) if row0 else Ms, tm)
    off = row0 // tm
    row = lambda i: (i, 0)
    return pl.pallas_call(
        _ffn_down_rows_kernel,
        out_shape=jax.ShapeDtypeStruct((Ms, D), f32),
        grid=(Ms // tm,),
        in_specs=[
            pl.BlockSpec((tm, D_FF), lambda i: (i + off, 0)),
            pl.BlockSpec((tm, D_FF), row),
            pl.BlockSpec((tm, D_FF), row),
            pl.BlockSpec((tm, D_FF), lambda i: (i + off, 1)),
            pl.BlockSpec((3, D_FF), lambda i: (0, 0)),
            pl.BlockSpec((1, D_FF), lambda i: (0, 0)),
            pl.BlockSpec((D_FF, D), lambda i: (0, 0)),
            pl.BlockSpec((tm, D), lambda i: (i + off, 0)),
        ],
        out_specs=pl.BlockSpec((tm, D), row),
        compiler_params=_cparams(("parallel",)),
        name="ffn_down_rows",
    )(up, a1, a2, up, conv_w, conv_b.reshape(1, D_FF), w_down, resid)


def _retention_kernel(q_ref, k_ref, v_ref, dm_ref, qd_ref, kd_ref, cd_ref, s0_ref, o_ref, so_ref, s_scr):
    c = pl.program_id(1)

    @pl.when(c == 0)
    def _():
        s_scr[...] = s0_ref[0]

    for h in range(RET_HEADS):
        lanes = slice(h * RET_D, (h + 1) * RET_D)
        q = q_ref[0, :, lanes]
        k = k_ref[0, :, lanes]
        v = v_ref[0, :, lanes].astype(bf16)
        S = s_scr[h]
        inner = _dot_nt(q.astype(bf16), k.astype(bf16)) * dm_ref[h]
        o_ref[0, :, lanes] = _dot(inner.astype(bf16), v) + _dot((q * qd_ref[h]).astype(bf16), S.astype(bf16))
        kd = (k * kd_ref[h]).T
        s_new = S * cd_ref[h, 0:1, :] + _dot(kd.astype(bf16), v)
        s_scr[h] = s_new
        so_ref[0, h] = s_new


def _retention(q, k, v, s0, length):
    Bn, Tn, _ = q.shape
    C = RET_D
    nc = Tn // C
    H = RET_HEADS
    log_g = jnp.log(1.0 - 2.0 ** (-5.0 - jnp.arange(H, dtype=f32)))
    idx = jnp.arange(C, dtype=f32)
    diff = idx[:, None] - idx[None, :]
    dmask = jnp.where(diff >= 0, jnp.exp(jnp.maximum(diff, 0.0) * log_g[:, None, None]), 0.0)
    q_dec = jnp.exp((idx + 1.0) * log_g[:, None])
    k_dec = jnp.where(idx < length, jnp.exp(jnp.maximum(length - 1.0 - idx, 0.0) * log_g[:, None]), 0.0)
    c_dec = jnp.exp(length * log_g)
    qd = jnp.broadcast_to(q_dec[:, :, None], (H, C, C))
    kd = jnp.broadcast_to(k_dec[:, :, None], (H, C, C))
    cd = jnp.broadcast_to(c_dec[:, None, None], (H, 8, C))
    qkv_spec = pl.BlockSpec((1, C, H * C), lambda b, c: (b, c, 0))
    tbl_spec = pl.BlockSpec((H, C, C), lambda b, c: (0, 0, 0))
    st_spec = pl.BlockSpec((1, H, C, C), lambda b, c: (b, 0, 0, 0))
    return pl.pallas_call(
        _retention_kernel,
        out_shape=(jax.ShapeDtypeStruct((Bn, Tn, H * C), f32), jax.ShapeDtypeStruct((Bn, H, C, C), f32)),
        grid=(Bn, nc),
        in_specs=[qkv_spec, qkv_spec, qkv_spec, tbl_spec, tbl_spec, tbl_spec,
                  pl.BlockSpec((H, 8, C), lambda b, c: (0, 0, 0)), st_spec],
        out_specs=(qkv_spec, st_spec),
        scratch_shapes=[pltpu.VMEM((H, C, C), f32)],
        compiler_params=_cparams(("parallel", "arbitrary")),
        name="retention",
    )(q, k, v, dmask, qd, kd, cd, s0)


def _pairs(nq, jlo, jhi):
    ii, jj, fl = [], [], []
    for i in range(nq):
        lo, hi = jlo(i), jhi(i)
        for j in range(lo, hi + 1):
            ii.append(i)
            jj.append(j)
            fl.append((1 if j == lo else 0) | (2 if j == hi else 0))
    return (jnp.asarray(np.array(ii, np.int32)), jnp.asarray(np.array(jj, np.int32)),
            jnp.asarray(np.array(fl, np.int32)))


def _add_bias_tiles(s, tbl_ref, i, tile0, nd):
    parts = []
    for c in range(s.shape[1] // LANES):
        d = i - (tile0 + c)
        idx = jnp.where(d < 0, nd, jnp.minimum(d, nd - 1))
        parts.append(s[:, c * LANES:(c + 1) * LANES] + tbl_ref[0, idx])
    return jnp.concatenate(parts, axis=1)


def _flash_init(m_s, l_s, acc_s):
    m_s[...] = jnp.full(m_s.shape, NEG, f32)
    l_s[...] = jnp.zeros(l_s.shape, f32)
    acc_s[...] = jnp.zeros(acc_s.shape, f32)


def _flash_update(s, v, m_s, l_s, acc_s):
    m_prev = m_s[...]
    m_new = jnp.maximum(m_prev, jnp.max(s, axis=-1, keepdims=True))
    alpha = jnp.exp(m_prev - m_new)
    pr = jnp.exp(s - m_new)
    l_s[...] = alpha * l_s[...] + jnp.sum(pr, axis=-1, keepdims=True)
    acc_s[...] = alpha * acc_s[...] + _dot(pr.astype(bf16), v)
    m_s[...] = m_new


def _mla_prompt_kernel(ii_ref, jj_ref, fl_ref, qn_ref, qp_ref, wuk_ref, ckv_ref, kr_ref, tbl_ref, wuv_ref, o_ref,
                       qa_s, qp_s, m_s, l_s, acc_s, *, tq, tk):
    p = pl.program_id(1)
    i = ii_ref[p]
    j = jj_ref[p]
    fl = fl_ref[p]
    H = MLA_HEADS

    @pl.when((fl & 1) == 1)
    def _():
        for h in range(H):
            qh = qn_ref[0, :, h * MLA_NOPE:(h + 1) * MLA_NOPE].astype(bf16)
            qa = _dot(qh, wuk_ref[h]) * MLA_SCALE
            qa_s[h * tq:(h + 1) * tq, :] = qa.astype(bf16)
            qp_s[h * tq:(h + 1) * tq, :] = qp_ref[0, :, h * MLA_ROPE:(h + 1) * MLA_ROPE].astype(bf16)
        _flash_init(m_s, l_s, acc_s)

    kv = ckv_ref[0].astype(bf16)
    kr = kr_ref[0].astype(bf16)
    s = _dot_nt(qa_s[...], kv) + _dot_nt(qp_s[...], kr)
    s = _add_bias_tiles(s, tbl_ref, i, j * (tk // LANES), tbl_ref.shape[1] - 1)
    _flash_update(s, kv, m_s, l_s, acc_s)

    @pl.when((fl & 2) == 2)
    def _():
        o = acc_s[...] / l_s[...]
        for h in range(H):
            o_ref[0, :, h * MLA_V:(h + 1) * MLA_V] = _dot(o[h * tq:(h + 1) * tq].astype(bf16), wuv_ref[h])


def _mla_prompt(qn, qp, ckv, kr, wuk_t, wuv, tk=FLASH_TK):
    B, T, _ = qn.shape
    tq = LANES
    tk = min(tk, T)
    H = MLA_HEADS
    R = H * tq
    ii, jj, fl = _pairs(T // tq, lambda i: 0, lambda i: (i * tq + tq - 1) // tk)
    npairs = int(ii.shape[0])
    r = np.arange(tq)[:, None]
    c = np.arange(LANES)[None, :]
    causal = np.where(c <= r, 0.0, NEG).astype(np.float32)
    tbl = np.stack([np.tile(causal, (H, 1)), np.zeros((R, LANES), np.float32), np.full((R, LANES), NEG, np.float32)])
    grid_spec = pltpu.PrefetchScalarGridSpec(
        num_scalar_prefetch=3,
        grid=(B, npairs),
        in_specs=[
            pl.BlockSpec((1, tq, H * MLA_NOPE), lambda b, p, ii, jj, fl: (b, ii[p], 0)),
            pl.BlockSpec((1, tq, H * MLA_ROPE), lambda b, p, ii, jj, fl: (b, ii[p], 0)),
            pl.BlockSpec((H, MLA_NOPE, MLA_KV_RANK), lambda b, p, ii, jj, fl: (0, 0, 0)),
            pl.BlockSpec((1, tk, MLA_KV_RANK), lambda b, p, ii, jj, fl: (b, jj[p], 0)),
            pl.BlockSpec((1, tk, MLA_ROPE), lambda b, p, ii, jj, fl: (b, jj[p], 0)),
            pl.BlockSpec((1, 3, R, LANES), lambda b, p, ii, jj, fl: (0, 0, 0, 0)),
            pl.BlockSpec((H, MLA_KV_RANK, MLA_V), lambda b, p, ii, jj, fl: (0, 0, 0)),
        ],
        out_specs=pl.BlockSpec((1, tq, H * MLA_V), lambda b, p, ii, jj, fl: (b, ii[p], 0)),
        scratch_shapes=[
            pltpu.VMEM((R, MLA_KV_RANK), bf16),
            pltpu.VMEM((R, MLA_ROPE), bf16),
            pltpu.VMEM((R, 1), f32),
            pltpu.VMEM((R, 1), f32),
            pltpu.VMEM((R, MLA_KV_RANK), f32),
        ],
    )
    return pl.pallas_call(
        functools.partial(_mla_prompt_kernel, tq=tq, tk=tk),
        out_shape=jax.ShapeDtypeStruct((B, T, H * MLA_V), f32),
        grid_spec=grid_spec,
        compiler_params=_cparams(("parallel", "arbitrary")),
        name="mla_prompt",
    )(ii, jj, fl, qn, qp, wuk_t, ckv, kr, jnp.asarray(tbl)[None], wuv)


def _page_copies(pt_ref, cache, buf, sem, layer, npg, seq, slot, transposed):
    cps = []
    for p in range(npg):
        src = cache.at[layer, pt_ref[seq * npg + p]]
        if len(buf.shape) == 4:
            dst = buf.at[slot, p]
        elif transposed:
            pw = cache.shape[3]
            dst = buf.at[slot, :, pl.ds(p * pw, pw)]
        else:
            pw = cache.shape[2]
            dst = buf.at[slot, pl.ds(p * pw, pw)]
        cps.append(pltpu.make_async_copy(src, dst, sem.at[slot]))
    return cps


def _gather_step(pt_ref, caches, bufs, sems, transposed, layer, npg):
    b = pl.program_id(0)
    nb = pl.num_programs(0)

    def copies(seq, slot):
        out = []
        for cache, buf, sem, tr in zip(caches, bufs, sems, transposed):
            out += _page_copies(pt_ref, cache, buf, sem, layer, npg, seq, slot, tr)
        return out

    @pl.when(b == 0)
    def _():
        for cp in copies(0, 0):
            cp.start()

    @pl.when(b + 1 < nb)
    def _():
        for cp in copies(b + 1, lax.rem(b + 1, 2)):
            cp.start()

    slot = lax.rem(b, 2)
    for cp in copies(b, slot):
        cp.wait()
    return slot


def _paged_attn_kernel(pt_ref, *refs, layer, npg, main_t, has2, has_bias, has_mask):
    it = iter(refs)
    q_ref = next(it)
    q2_ref = next(it) if has2 else None
    bias_ref = next(it) if has_bias else None
    mask_ref = next(it) if has_mask else None
    xk_ref = next(it)
    xk2_ref = next(it) if has2 else None
    xb_ref = next(it)
    cache = next(it)
    cache2 = next(it) if has2 else None
    o_ref = next(it)
    buf = next(it)
    buf2 = next(it) if has2 else None
    sem = next(it)
    sem2 = next(it) if has2 else None

    caches, bufs, sems, trs = [cache], [buf], [sem], [main_t]
    if has2:
        caches.append(cache2)
        bufs.append(buf2)
        sems.append(sem2)
        trs.append(True)
    slot = _gather_step(pt_ref, caches, bufs, sems, trs, layer, npg)

    q = q_ref[0].astype(bf16)
    pages = buf[slot].astype(bf16)
    xk = xk_ref[0].astype(bf16)
    s = _dot(q, pages) if main_t else _dot_nt(q, pages)
    sx = _dot_nt(q, xk)
    if has2:
        q2 = q2_ref[0].astype(bf16)
        s = s + _dot(q2, buf2[slot].astype(bf16))
        sx = sx + _dot_nt(q2, xk2_ref[0].astype(bf16))
    if has_bias:
        s = s + bias_ref[...]
    if has_mask:
        reps = s.shape[0] // mask_ref.shape[1]
        s = s + jnp.concatenate([mask_ref[0]] * reps, axis=0)
    sx = sx + xb_ref[0]
    m = jnp.maximum(jnp.max(s, axis=-1, keepdims=True), jnp.max(sx, axis=-1, keepdims=True))
    pr = jnp.exp(s - m)
    px = jnp.exp(sx - m)
    l = jnp.sum(pr, axis=-1, keepdims=True) + jnp.sum(px, axis=-1, keepdims=True)
    pb = pr.astype(bf16)
    o = (_dot_nt(pb, pages) if main_t else _dot(pb, pages)) + _dot(px.astype(bf16), xk)
    o_ref[0] = jnp.where(m > 0.5 * NEG, o / l, 0.0)


def _paged_attn(page_table, cache, layer, q, xk, xb, main_t, q2=None, cache2=None, xk2=None, bias=None, mask=None):
    DB, R, DL = q.shape
    npg = page_table.shape[1]
    pw = cache.shape[3] if main_t else cache.shape[2]
    K = npg * pw
    has2 = q2 is not None
    seq = lambda b, pt: (b, 0, 0)
    in_specs = [pl.BlockSpec((1, R, DL), seq)]
    args = [q]
    if has2:
        d2 = q2.shape[-1]
        in_specs.append(pl.BlockSpec((1, R, d2), seq))
        args.append(q2)
    if bias is not None:
        in_specs.append(pl.BlockSpec((R, K), lambda b, pt: (0, 0)))
        args.append(bias)
    if mask is not None:
        in_specs.append(pl.BlockSpec((1, 8, K), seq))
        args.append(mask)
    in_specs.append(pl.BlockSpec((1, 8, DL), seq))
    args.append(xk)
    if has2:
        in_specs.append(pl.BlockSpec((1, 8, d2), seq))
        args.append(xk2)
    if xb.shape[0] == 1:
        in_specs.append(pl.BlockSpec((1, R, 8), lambda b, pt: (0, 0, 0)))
    else:
        in_specs.append(pl.BlockSpec((1, R, 8), seq))
    args.append(xb)
    in_specs.append(pl.BlockSpec(memory_space=pl.ANY))
    args.append(cache)
    scratch = [pltpu.VMEM((2, DL, K) if main_t else (2, K, DL), f32)]
    if has2:
        in_specs.append(pl.BlockSpec(memory_space=pl.ANY))
        args.append(cache2)
        scratch.append(pltpu.VMEM((2, d2, K), f32))
    scratch.append(pltpu.SemaphoreType.DMA((2,)))
    if has2:
        scratch.append(pltpu.SemaphoreType.DMA((2,)))
    grid_spec = pltpu.PrefetchScalarGridSpec(
        num_scalar_prefetch=1,
        grid=(DB,),
        in_specs=in_specs,
        out_specs=pl.BlockSpec((1, R, DL), seq),
        scratch_shapes=scratch,
    )
    return pl.pallas_call(
        functools.partial(_paged_attn_kernel, layer=layer, npg=npg, main_t=main_t, has2=has2,
                          has_bias=bias is not None, has_mask=mask is not None),
        out_shape=jax.ShapeDtypeStruct((DB, R, DL), f32),
        grid_spec=grid_spec,
        compiler_params=_cparams(("arbitrary",)),
        name="paged_attn",
    )(page_table.reshape(-1), *args)


def _compress_compute(load_rows, pe_ref, w1_ref, w2_ref, o_ref, nblk, W):
    half = nblk // 2
    acc = jnp.zeros((nblk, KV_LANES), f32)
    for r in range(CMP_BLOCK):
        x = jnp.concatenate([load_rows(r, half), load_rows(CMP_BLOCK + r, half)], axis=0) + pe_ref[r:r + 1, :]
        acc = acc + _dot(x.astype(bf16), w1_ref[r])
    hdn = acc * jax.nn.sigmoid(acc)
    out = _dot(hdn.astype(bf16), w2_ref[...])
    o_ref[0, 0:half, :] = out[:half]
    o_ref[0, W:W + half, :] = out[half:]
    if half < W:
        zeros = jnp.zeros((W - half, KV_LANES), f32)
        o_ref[0, half:W, :] = zeros
        o_ref[0, W + half:2 * W, :] = zeros


def _compress_kernel(xlo_ref, xhi_ref, pe_ref, w1_ref, w2_ref, o_ref, *, nblk, W):
    def load(r, n):
        rows = pl.ds(r, n, stride=2 * CMP_BLOCK)
        return jnp.concatenate([xlo_ref[0, rows, :], xhi_ref[0, rows, :]], axis=1)

    _compress_compute(load, pe_ref, w1_ref, w2_ref, o_ref, nblk, W)


def _compress_weight_specs(idx):
    return [
        pl.BlockSpec((CMP_BLOCK, KV_LANES), idx(2)),
        pl.BlockSpec((CMP_BLOCK, KV_LANES, KV_LANES), idx(3)),
        pl.BlockSpec((KV_LANES, KV_LANES), idx(2)),
    ]


def _compress(z3, colblk, pe_row, w1bd, w2p, W):
    B, T, _ = z3.shape
    nblk = T // CMP_BLOCK
    return pl.pallas_call(
        functools.partial(_compress_kernel, nblk=nblk, W=W),
        out_shape=jax.ShapeDtypeStruct((B, 2 * W, KV_LANES), f32),
        grid=(B,),
        in_specs=[pl.BlockSpec((1, T, LANES), lambda b: (b, 0, 2 * colblk)),
                  pl.BlockSpec((1, T, LANES), lambda b: (b, 0, 2 * colblk + 1))]
        + _compress_weight_specs(lambda n: (lambda b: (0,) * n)),
        out_specs=pl.BlockSpec((1, 2 * W, KV_LANES), lambda b: (b, 0, 0)),
        compiler_params=_cparams(("parallel",)),
        name="compress",
    )(z3, z3, pe_row, w1bd, w2p)


def _compress_paged_kernel(pt_ref, pe_ref, w1_ref, w2_ref, cache, o_ref, buf, xs, sem, *, layer, npg, W):
    slot = _gather_step(pt_ref, [cache], [buf], [sem], [True], layer, npg)

    group = 2 * CMP_BLOCK

    def untranspose(p, carry):
        for sg in range(KV_LANES // LANES):
            t = buf[slot, p, sg * LANES:(sg + 1) * LANES, :].T
            for gi in range(PAGE // group):
                r0 = pl.multiple_of((p * (PAGE // group) + gi) * XS_PITCH, 8)
                xs[sg, pl.ds(r0, group), :] = t[gi * group:(gi + 1) * group]
        return carry

    lax.fori_loop(0, npg, untranspose, 0)

    def load(r, n):
        rows = pl.ds(r, n, stride=XS_PITCH)
        return jnp.concatenate([xs[sg, rows, :] for sg in range(KV_LANES // LANES)], axis=1)

    _compress_compute(load, pe_ref, w1_ref, w2_ref, o_ref, npg * PAGE // CMP_BLOCK, W)


def _compress_paged(page_table, cache_t, layer, pe_row, w1bd, w2p, W):
    DB, npg = page_table.shape
    grid_spec = pltpu.PrefetchScalarGridSpec(
        num_scalar_prefetch=1,
        grid=(DB,),
        in_specs=_compress_weight_specs(lambda n: (lambda b, pt: (0,) * n)) + [pl.BlockSpec(memory_space=pl.ANY)],
        out_specs=pl.BlockSpec((1, 2 * W, KV_LANES), lambda b, pt: (b, 0, 0)),
        scratch_shapes=[pltpu.VMEM((2, npg, KV_LANES, PAGE), f32),
                        pltpu.VMEM((KV_LANES // LANES, npg * PAGE // (2 * CMP_BLOCK) * XS_PITCH, LANES), f32),
                        pltpu.SemaphoreType.DMA((2,))],
    )
    return pl.pallas_call(
        functools.partial(_compress_paged_kernel, layer=layer, npg=npg, W=W),
        out_shape=jax.ShapeDtypeStruct((DB, 2 * W, KV_LANES), f32),
        grid_spec=grid_spec,
        compiler_params=_cparams(("arbitrary",)),
        name="compress_paged",
    )(page_table.reshape(-1), pe_row, w1bd, w2p, cache_t)


def _masked_softmax(s):
    valid = s > 0.5 * NEG
    m = jnp.max(s, axis=-1, keepdims=True)
    e = jnp.where(valid, jnp.exp(s - m), 0.0)
    l = jnp.sum(e, axis=-1, keepdims=True)
    return e * jnp.where(l > 0.0, 1.0 / l, 0.0)


def _select_blocks(imp, q_pos, n_blk):
    R, W = imp.shape
    blk = lax.broadcasted_iota(jnp.int32, (R, W), 1)
    cur = q_pos // SEL_BLOCK
    forced = (blk == 0) | (blk == cur) | (blk == cur - 1)
    valid = (blk * SEL_BLOCK <= q_pos) & (blk < n_blk)
    score = jnp.where(valid, jnp.where(forced, FORCE, imp), -1.0)
    score = jnp.where(blk < n_blk, score, -jnp.inf)
    rank = jnp.zeros((R, W), f32)
    for c in range(n_blk):
        col = score[:, c:c + 1]
        beats = (col > score) | ((col == score) & (blk > c))
        rank = rank + jnp.where(beats, 1.0, 0.0)
    return jnp.where((rank < float(min(N_SEL, n_blk))) & valid, 1.0, 0.0)


def _nsa_cmp_prompt_kernel(q_ref, kv_ref, bias_ref, oc_ref, sel_ref, *, tq, W, n_blk):
    i = pl.program_id(2)
    kv = kv_ref[0]
    k = kv[:, :NSA_DH].astype(bf16)
    v = kv[:, NSA_DH:].astype(bf16)
    imp = jnp.zeros((tq, 2 * W), f32)
    for h in range(NSA_HPG):
        qh = (q_ref[0, :, h * NSA_DH:(h + 1) * NSA_DH] * NSA_SCALE).astype(bf16)
        p = _masked_softmax(_dot_nt(qh, k) + bias_ref[h])
        oc_ref[0, :, h * NSA_DH:(h + 1) * NSA_DH] = _dot(p.astype(bf16), v)
        imp = imp + p
    imp = imp[:, :W] + imp[:, W:]
    q_pos = i * tq + lax.broadcasted_iota(jnp.int32, (tq, 1), 0)
    sel_ref[0, 0] = _select_blocks(imp, q_pos, n_blk)


def _nsa_cmp_prompt(z3, kcv, bias, W, n_blk, tq=LANES):
    B, T, _ = z3.shape
    return pl.pallas_call(
        functools.partial(_nsa_cmp_prompt_kernel, tq=tq, W=W, n_blk=n_blk),
        out_shape=(jax.ShapeDtypeStruct((B, T, NSA_HEADS * NSA_DH), f32),
                   jax.ShapeDtypeStruct((B, NSA_G, T, W), f32)),
        grid=(B, NSA_G, T // tq),
        in_specs=[
            pl.BlockSpec((1, tq, NSA_HPG * NSA_DH), lambda b, g, i: (b, i, g)),
            pl.BlockSpec((1, 2 * W, 2 * NSA_DH), lambda b, g, i: (b, 0, g)),
            pl.BlockSpec((NSA_HPG, tq, 2 * W), lambda b, g, i: (g, i, 0)),
        ],
        out_specs=(pl.BlockSpec((1, tq, NSA_HPG * NSA_DH), lambda b, g, i: (b, i, g)),
                   pl.BlockSpec((1, 1, tq, W), lambda b, g, i: (b, g, i, 0))),
        compiler_params=_cparams(("parallel", "parallel", "parallel")),
        name="nsa_cmp_prompt",
    )(z3, kcv, bias)


def _nsa_cmp_sample_kernel(q_ref, kv_ref, bias_ref, e_ref, oc_ref, sel_ref, mexp_ref, imp_s, *,
                           bt, ts, W, n_blk, past):
    for bb in range(bt):
        for g in range(NSA_G):
            k = kv_ref[bb, :, g * 2 * NSA_DH:g * 2 * NSA_DH + NSA_DH].astype(bf16)
            v = kv_ref[bb, :, g * 2 * NSA_DH + NSA_DH:(g + 1) * 2 * NSA_DH].astype(bf16)
            qg = (q_ref[bb, g] * NSA_SCALE).astype(bf16)
            p = _masked_softmax(_dot_nt(qg, k) + bias_ref[g])
            oc_ref[bb, g] = _dot(p.astype(bf16), v)
            imp = jnp.sum(p.reshape(ts, NSA_HPG, 2 * W), axis=1)
            r0 = (bb * NSA_G + g) * ts
            imp_s[r0:r0 + ts, :] = imp[:, :W] + imp[:, W:]
    rows = bt * NSA_G * ts
    q_pos = past + lax.rem(lax.broadcasted_iota(jnp.int32, (rows, 1), 0), ts)
    sel = _select_blocks(imp_s[...], q_pos, n_blk)
    sel_ref[...] = sel.reshape(bt, NSA_G * ts, W)
    hit = _dot(sel.astype(bf16), e_ref[...])
    mexp_ref[...] = jnp.where(hit > 0.5, 0.0, NEG).reshape(bt, NSA_G * ts, mexp_ref.shape[2])


def _nsa_cmp_sample(q, kcv, bias, expand, W, n_blk, past, bt=8):
    DB, _, R, _ = q.shape
    ts = R // NSA_HPG
    K = expand.shape[1]
    rows = NSA_G * ts
    return pl.pallas_call(
        functools.partial(_nsa_cmp_sample_kernel, bt=bt, ts=ts, W=W, n_blk=n_blk, past=past),
        out_shape=(jax.ShapeDtypeStruct((DB, NSA_G, R, NSA_DH), f32),
                   jax.ShapeDtypeStruct((DB, rows, W), f32),
                   jax.ShapeDtypeStruct((DB, rows, K), f32)),
        grid=(DB // bt,),
        in_specs=[
            pl.BlockSpec((bt, NSA_G, R, NSA_DH), lambda b: (b, 0, 0, 0)),
            pl.BlockSpec((bt, 2 * W, KV_LANES), lambda b: (b, 0, 0)),
            pl.BlockSpec((NSA_G, R, 2 * W), lambda b: (0, 0, 0)),
            pl.BlockSpec((W, K), lambda b: (0, 0)),
        ],
        out_specs=(pl.BlockSpec((bt, NSA_G, R, NSA_DH), lambda b: (b, 0, 0, 0)),
                   pl.BlockSpec((bt, rows, W), lambda b: (b, 0, 0)),
                   pl.BlockSpec((bt, rows, K), lambda b: (b, 0, 0))),
        scratch_shapes=[pltpu.VMEM((bt * rows, W), f32)],
        compiler_params=_cparams(("parallel",)),
        name="nsa_cmp_sample",
    )(q, kcv, bias, expand)


def _stack_heads(q_ref, q_s, tq):
    q_s[:, 0:2 * NSA_DH] = jnp.zeros((q_s.shape[0], 2 * NSA_DH), bf16)
    for h in range(NSA_HPG):
        q_s[h * tq:(h + 1) * tq, 0:NSA_DH] = (q_ref[0, :, h * NSA_DH:(h + 1) * NSA_DH] * NSA_SCALE).astype(bf16)


def _unstack_heads(o, o_ref, tq):
    for h in range(NSA_HPG):
        o_ref[0, :, h * NSA_DH:(h + 1) * NSA_DH] = o[h * tq:(h + 1) * tq, NSA_DH:]


def _nsa_sel_kernel(ii_ref, jj_ref, fl_ref, q_ref, kv_ref, tbl_ref, sel_ref, et_ref, o_ref,
                    q_s, m_s, l_s, acc_s, *, tq, tk):
    p = pl.program_id(2)
    i = ii_ref[p]
    j = jj_ref[p]
    fl = fl_ref[p]

    @pl.when((fl & 1) == 1)
    def _():
        _stack_heads(q_ref, q_s, tq)
        selneg = jnp.where(sel_ref[0, 0] > 0.5, 0.0, NEG).astype(bf16)
        for h in range(NSA_HPG):
            q_s[h * tq:(h + 1) * tq, 2 * NSA_DH:] = selneg
        _flash_init(m_s, l_s, acc_s)

    kvb = kv_ref[0].astype(bf16)
    s = _dot_nt(q_s[...], jnp.concatenate([kvb, et_ref[...]], axis=1))
    s = _add_bias_tiles(s, tbl_ref, i, j * (tk // LANES), tbl_ref.shape[1] - 1)
    _flash_update(s, kvb, m_s, l_s, acc_s)

    @pl.when((fl & 2) == 2)
    def _():
        _unstack_heads(acc_s[...] / l_s[...], o_ref, tq)


def _nsa_sel(z3, kvcol, tbl, sel, expand_t, tk=FLASH_TK):
    B, T, _ = z3.shape
    tq = LANES
    tk = min(tk, T)
    W = sel.shape[-1]
    R = NSA_HPG * tq
    ii, jj, fl = _pairs(T // tq, lambda i: 0, lambda i: (i * tq + tq - 1) // tk)
    npairs = int(ii.shape[0])
    grid_spec = pltpu.PrefetchScalarGridSpec(
        num_scalar_prefetch=3,
        grid=(B, NSA_G, npairs),
        in_specs=[
            pl.BlockSpec((1, tq, NSA_HPG * NSA_DH), lambda b, g, p, ii, jj, fl: (b, ii[p], g)),
            pl.BlockSpec((1, tk, 2 * NSA_DH), lambda b, g, p, ii, jj, fl: (b, jj[p], kvcol + g)),
            pl.BlockSpec((1, tbl.shape[1], R, LANES), lambda b, g, p, ii, jj, fl: (g, 0, 0, 0)),
            pl.BlockSpec((1, 1, tq, W), lambda b, g, p, ii, jj, fl: (b, g, ii[p], 0)),
            pl.BlockSpec((tk, W), lambda b, g, p, ii, jj, fl: (jj[p], 0)),
        ],
        out_specs=pl.BlockSpec((1, tq, NSA_HPG * NSA_DH), lambda b, g, p, ii, jj, fl: (b, ii[p], g)),
        scratch_shapes=[pltpu.VMEM((R, 2 * NSA_DH + W), bf16), pltpu.VMEM((R, 1), f32), pltpu.VMEM((R, 1), f32),
                        pltpu.VMEM((R, 2 * NSA_DH), f32)],
    )
    return pl.pallas_call(
        functools.partial(_nsa_sel_kernel, tq=tq, tk=tk),
        out_shape=jax.ShapeDtypeStruct((B, T, NSA_HEADS * NSA_DH), f32),
        grid_spec=grid_spec,
        compiler_params=_cparams(("parallel", "parallel", "arbitrary")),
        name="nsa_sel",
    )(ii, jj, fl, z3, z3, tbl, sel, expand_t)


def _nsa_win_kernel(q_ref, kv_ref, tbl_ref, o_ref, q_s, *, tq, nsub):
    i = pl.program_id(2)
    _stack_heads(q_ref, q_s, tq)
    tile0 = jnp.maximum(i - (nsub - 1), 0)
    kvb = kv_ref[0, pl.ds(pl.multiple_of(tile0 * LANES, LANES), nsub * LANES), :].astype(bf16)
    s = _add_bias_tiles(_dot_nt(q_s[...], kvb), tbl_ref, i, tile0, tbl_ref.shape[1] - 1)
    m = jnp.max(s, axis=-1, keepdims=True)
    pr = jnp.exp(s - m)
    l = jnp.sum(pr, axis=-1, keepdims=True)
    _unstack_heads(_dot(pr.astype(bf16), kvb) / l, o_ref, tq)


def _nsa_win(z3, kvcol, tbl):
    B, T, _ = z3.shape
    tq = LANES
    nsub = tbl.shape[1] - 1
    assert T >= nsub * LANES
    R = NSA_HPG * tq
    return pl.pallas_call(
        functools.partial(_nsa_win_kernel, tq=tq, nsub=nsub),
        out_shape=jax.ShapeDtypeStruct((B, T, NSA_HEADS * NSA_DH), f32),
        grid=(B, NSA_G, T // tq),
        in_specs=[
            pl.BlockSpec((1, tq, NSA_HPG * NSA_DH), lambda b, g, i: (b, i, g)),
            pl.BlockSpec((1, T, 2 * NSA_DH), lambda b, g, i: (b, 0, kvcol + g)),
            pl.BlockSpec((1, nsub + 1, R, LANES), lambda b, g, i: (g, 0, 0, 0)),
        ],
        out_specs=pl.BlockSpec((1, tq, NSA_HPG * NSA_DH), lambda b, g, i: (b, i, g)),
        scratch_shapes=[pltpu.VMEM((R, 2 * NSA_DH), bf16)],
        compiler_params=_cparams(("parallel", "parallel", "parallel")),
        name="nsa_win",
    )(z3, z3, tbl)


def _bucket_np(n):
    exact = N_BUCKETS // 2
    nf = np.maximum(n, 1).astype(np.float32)
    large = exact + (np.log(nf / exact) / math.log(MAX_DISTANCE / exact) * (N_BUCKETS - exact)).astype(np.int32)
    return np.where(n < exact, n, np.minimum(large, N_BUCKETS - 1)).astype(np.int32)


def _bucket_starts():
    b = _bucket_np(np.arange(4 * MAX_DISTANCE))
    return [int(np.argmax(b >= k)) for k in range(N_BUCKETS)]


def _t5_bias(rel_bias, dist, mask):
    n = jnp.maximum(dist, 0)[None]
    tbl = rel_bias.astype(f32).T
    col = lambda k: tbl[:, k].reshape((NSA_HEADS,) + (1,) * dist.ndim)
    out = jnp.broadcast_to(col(0), (NSA_HEADS,) + dist.shape)
    for k, start in enumerate(_bucket_starts()):
        if k:
            out = jnp.where(n >= start, col(k), out)
    return jnp.where(mask[None], out, NEG)


def _cmp_block_of_lane(W, n_cmp):
    lane = np.arange(2 * W)
    half = n_cmp // 2
    blk = np.where(lane < W, 2 * lane, 2 * (lane - W) + 1)
    ok = np.where(lane < W, lane < half, (lane - W) < half)
    return np.where(ok, blk, -1)


def _rope_tables(pos, half):
    inv = ROPE_BASE ** (-jnp.arange(half, dtype=f32) / half)
    ang = pos.astype(f32)[:, None] * inv[None, :]
    return jnp.cos(ang)[:, None, :], jnp.sin(ang)[:, None, :]


def _rope(x, cs):
    cos, sin = cs
    half = x.shape[-1] // 2
    x1, x2 = x[..., :half], x[..., half:]
    return jnp.concatenate([x1 * cos - x2 * sin, x2 * cos + x1 * sin], axis=-1)


def _block_diag(blocks):
    n, r, c = blocks.shape[-3:]
    eye = jnp.eye(n, dtype=blocks.dtype)
    out = blocks[..., :, :, None, :] * eye[:, None, :, None]
    return out.reshape(blocks.shape[:-3] + (n * r, n * c))


def _pages_t(cache):
    L, pool, rows = cache.shape[:3]
    nd = cache.ndim
    return cache.transpose((0, 1) + tuple(range(3, nd)) + (2,)).reshape(L, pool, -1, rows)


def kernel(x_prompt, x_sample, cache_mla_ckv, cache_mla_krope, cache_nsa_cmp, cache_nsa_sel, page_table, state_nsa_win, state_ret, state_ffn_conv, rel_bias, norm_mix, norm_ffn, final_norm, ev_w_in, mla_q_norm, mla_w_qb, mla_kv_norm, mla_w_uk, mla_w_uv, ev_w_out, od_w_in, cmp_pe, cmp_w1, cmp_w2, od_w_out, ffn_w_up, ffn_conv_w, ffn_conv_b, ffn_w_down):
    B, T, D = x_prompt.shape
    DB, TS, _ = x_sample.shape
    depth = norm_mix.shape[0]
    npg = page_table.shape[1]
    past = npg * PAGE
    Mp, Ms = B * T, DB * TS
    assert T % LANES == 0 and NSA_G * TS == 8 and past % (2 * CMP_BLOCK) == 0 and TS < CMP_BLOCK
    assert past % SEL_BLOCK + TS <= SEL_BLOCK

    pos_p = jnp.arange(T)
    pos_s = past + jnp.arange(TS)
    pos_all = jnp.concatenate([jnp.tile(pos_p, B), jnp.tile(pos_s, DB)])
    rope16 = _rope_tables(pos_all, MLA_ROPE // 2)
    rope64 = _rope_tables(pos_all, RET_D // 2)

    h = jnp.concatenate([x_prompt.reshape(Mp, D), x_sample.reshape(Ms, D)], axis=0)

    tq = LANES
    n_blk_p = T // SEL_BLOCK
    Wp = 128
    assert n_blk_p <= Wp
    lane_blk = _cmp_block_of_lane(Wp, T // CMP_BLOCK)
    end = jnp.asarray(lane_blk * CMP_BLOCK + (CMP_BLOCK - 1))
    dist = jnp.arange(T)[:, None] - end[None, :]
    bias_cmp_p = _t5_bias(rel_bias, dist, (dist >= 0) & jnp.asarray(lane_blk >= 0)[None, :])

    def flash_table(nd, window):
        r = jnp.arange(tq)[None, :, None]
        c = jnp.arange(LANES)[None, None, :]
        dd = jnp.arange(nd + 1)[:, None, None] * LANES + r - c
        mask = (dd >= 0) & (jnp.arange(nd + 1)[:, None, None] < nd)
        if window:
            mask &= dd < WINDOW
        t = _t5_bias(rel_bias, dd, mask)
        t = t.reshape(NSA_G, NSA_HPG, nd + 1, tq, LANES).transpose(0, 2, 1, 3, 4)
        return t.reshape(NSA_G, nd + 1, NSA_HPG * tq, LANES)

    tbl_sel = flash_table(3, False)
    tbl_win = flash_table(WINDOW // LANES + 1, True)
    expand_pt = (jnp.arange(T)[:, None] // SEL_BLOCK == jnp.arange(Wp)[None, :]).astype(bf16)

    n_blk_s = -(-(past + TS) // SEL_BLOCK)
    Ws = max(128, -(-n_blk_s // 128) * 128)
    n_cmp_s = past // CMP_BLOCK
    assert n_cmp_s // 2 <= Ws
    lane_blk_s = _cmp_block_of_lane(Ws, n_cmp_s)
    end_s = jnp.asarray(lane_blk_s * CMP_BLOCK + (CMP_BLOCK - 1))
    qpos_s = past + jnp.arange(TS)
    dist_s = qpos_s[:, None] - end_s[None, :]
    bias_cmp_s = _t5_bias(rel_bias, dist_s, (dist_s >= 0) & jnp.asarray(lane_blk_s >= 0)[None, :])
    bias_cmp_s = bias_cmp_s.reshape(NSA_G, NSA_HPG, TS, 2 * Ws).transpose(0, 2, 1, 3).reshape(NSA_G, TS * NSA_HPG, 2 * Ws)
    expand_s = (jnp.arange(Ws)[:, None] == jnp.arange(past)[None, :] // SEL_BLOCK).astype(bf16)

    def sample_rows_table(kpos, window):
        dd = qpos_s[:, None] - kpos[None, :]
        mask = dd >= 0
        if window:
            mask &= (dd < WINDOW) & (kpos >= 0)[None, :]
        t = _t5_bias(rel_bias, dd, mask)
        t = t.reshape(NSA_G, NSA_HPG, TS, -1).transpose(1, 0, 2, 3)
        return t.reshape(NSA_HPG * NSA_G * TS, -1)

    bias_sel_s = sample_rows_table(jnp.arange(past), False)
    wb = state_nsa_win.shape[2]
    bias_win_s = sample_rows_table(past - wb + jnp.arange(wb), True)
    new_pos = jnp.concatenate([qpos_s, jnp.full((8 - TS,), 10 ** 9, qpos_s.dtype)])
    xb_nsa = sample_rows_table(new_pos, False)
    win_pages = jnp.arange(DB, dtype=jnp.int32).reshape(DB, 1)

    tt = np.repeat(np.arange(TS), MLA_HEADS)
    xb_mla = jnp.asarray(np.where(np.arange(8)[None, :] <= tt[:, None], 0.0, NEG).astype(np.float32))[None]

    krope_t = _pages_t(cache_mla_krope)
    cmp_t = _pages_t(cache_nsa_cmp)
    sel_t = _pages_t(cache_nsa_sel)
    win_t = _pages_t(state_nsa_win)

    outs = {k: [] for k in ("ckv_p", "ckv_s", "kr_p", "kr_s", "cmp_p", "cmp_s", "sel_p", "sel_s",
                            "win_p", "win_s", "ret_p", "ret_s", "conv_p", "conv_s")}

    def split(a):
        return a[:Mp], a[Mp:]

    def pad_rows(a, n):
        return jnp.pad(a, ((0, 0), (0, n - a.shape[1]), (0, 0)))

    for layer in range(depth):
        j = layer // 2
        if layer % 2 == 0:
            w = ev_w_in[j]
            a0 = MLA_Q_RANK + MLA_KV_RANK
            c0 = a0 + MLA_ROPE
            w_p = jnp.concatenate([w[:, :a0], w[:, c0:], w[:, a0:c0], jnp.zeros((D, 96), f32)], axis=1).astype(bf16)
            z = _mm(h, w_p, g=norm_mix[layer])
            q = _mm(z, mla_w_qb[j].astype(bf16), g=mla_q_norm[j])
            zc = z[:, MLA_Q_RANK:a0]
            ckv = zc * lax.rsqrt(jnp.mean(zc * zc, axis=-1, keepdims=True) + EPS) * mla_kv_norm[j]
            r0 = a0
            nr = RET_HEADS * RET_D
            kr = _rope(z[:, r0 + 4 * nr:r0 + 4 * nr + MLA_ROPE][:, None, :], rope16)[:, 0]
            q3 = q.reshape(-1, MLA_HEADS, MLA_NOPE + MLA_ROPE)
            qn = q3[..., :MLA_NOPE].reshape(-1, MLA_HEADS * MLA_NOPE)
            qp = (_rope(q3[..., MLA_NOPE:], rope16) * MLA_SCALE).reshape(-1, MLA_HEADS * MLA_ROPE)
            rq = _rope(z[:, r0:r0 + nr].reshape(-1, RET_HEADS, RET_D), rope64).reshape(-1, nr)
            rk = (_rope(z[:, r0 + nr:r0 + 2 * nr].reshape(-1, RET_HEADS, RET_D), rope64) * RET_D ** -0.5).reshape(-1, nr)
            rv = z[:, r0 + 2 * nr:r0 + 3 * nr]
            rg = z[:, r0 + 3 * nr:r0 + 4 * nr]

            wuk_t = mla_w_uk[j].transpose(1, 2, 0).astype(bf16)
            wuv = mla_w_uv[j].transpose(1, 0, 2).astype(bf16)
            qn_p, qn_s = split(qn)
            qp_p, qp_s = split(qp)
            ckv_p, ckv_s = split(ckv)
            kr_p, kr_s = split(kr)

            o_mla_p = _mla_prompt(qn_p.reshape(B, T, -1), qp_p.reshape(B, T, -1), ckv_p.reshape(B, T, -1),
                                  kr_p.reshape(B, T, -1), wuk_t, wuv).reshape(Mp, -1)
            wuk_bd = _block_diag(wuk_t)
            qa_s = (_mm(qn_s, wuk_bd) * MLA_SCALE).reshape(DB, TS * MLA_HEADS, MLA_KV_RANK)
            o_lat = _paged_attn(page_table, cache_mla_ckv, j, qa_s,
                                pad_rows(ckv_s.reshape(DB, TS, -1), 8), xb_mla, False,
                                q2=qp_s.reshape(DB, TS * MLA_HEADS, MLA_ROPE), cache2=krope_t,
                                xk2=pad_rows(kr_s.reshape(DB, TS, -1), 8))
            wuv_bd = _block_diag(wuv)
            o_mla_s = _mm(o_lat.reshape(Ms, MLA_HEADS * MLA_KV_RANK), wuv_bd)

            rq_p, rq_s = split(rq)
            rk_p, rk_s = split(rk)
            rv_p, rv_s = split(rv)
            ro_p, rs_p = _retention(rq_p.reshape(B, T, nr), rk_p.reshape(B, T, nr), rv_p.reshape(B, T, nr),
                                    jnp.zeros((B, RET_HEADS, RET_D, RET_D), f32), float(min(RET_D, T)))
            ro_s, rs_s = _retention(pad_rows(rq_s.reshape(DB, TS, nr), RET_D), pad_rows(rk_s.reshape(DB, TS, nr), RET_D),
                                    pad_rows(rv_s.reshape(DB, TS, nr), RET_D), state_ret[j], float(TS))
            ro = jnp.concatenate([ro_p.reshape(Mp, nr), ro_s[:, :TS].reshape(Ms, nr)], axis=0)
            ro = ro.reshape(-1, RET_HEADS, RET_D)
            rn = ro * lax.rsqrt(jnp.mean(ro * ro, axis=-1, keepdims=True) + EPS)
            o_ret = jax.nn.silu(rg) * rn.reshape(-1, nr)
            mix = jnp.concatenate([jnp.concatenate([o_mla_p, o_mla_s], axis=0), o_ret], axis=1)
            h = _mm(mix, ev_w_out[j].astype(bf16), resid=h)

            outs["ckv_p"].append(ckv_p.reshape(B, T, -1))
            outs["ckv_s"].append(ckv_s.reshape(DB, TS, -1))
            outs["kr_p"].append(kr_p.reshape(B, T, -1))
            outs["kr_s"].append(kr_s.reshape(DB, TS, -1))
            outs["ret_p"].append(rs_p)
            outs["ret_s"].append(rs_s)
        else:
            w = od_w_in[j]
            nq_cols = NSA_HEADS * NSA_DH
            wkv = w[:, nq_cols:nq_cols + 3 * KV_LANES].reshape(D, 3, 2, NSA_G, NSA_DH)
            wsel = wkv[:, 1].transpose(0, 2, 1, 3).reshape(D, KV_LANES)
            wwin = wkv[:, 2].transpose(0, 2, 1, 3).reshape(D, KV_LANES)
            ngate = 3 * NSA_HEADS
            w_p = jnp.concatenate([w[:, :nq_cols + KV_LANES], wsel, wwin, w[:, nq_cols + 3 * KV_LANES:],
                                   jnp.zeros((D, 128 - ngate), f32)], axis=1).astype(bf16)
            z = _mm(h, w_p, g=norm_mix[layer])
            NZ = z.shape[1]
            c_cmp = nq_cols
            c_sel = c_cmp + KV_LANES
            c_win = c_sel + KV_LANES
            c_gate = c_win + KV_LANES
            z_p, z_s = split(z)
            z3 = z_p.reshape(B, T, NZ)
            gates = jax.nn.sigmoid(z[:, c_gate:c_gate + ngate]).reshape(-1, NSA_HEADS, 3)

            def orig_kv(a):
                return a.reshape(a.shape[0], NSA_G, 2, NSA_DH).transpose(0, 2, 1, 3)

            cmp_rows = z[:, c_cmp:c_sel].reshape(-1, 2, NSA_G, NSA_DH)
            sel_rows = orig_kv(z[:, c_sel:c_win])
            win_rows = orig_kv(z[:, c_win:c_gate])

            w1k = cmp_w1[j, 0].reshape(CMP_BLOCK, NSA_DH, -1)
            w1v = cmp_w1[j, 1].reshape(CMP_BLOCK, NSA_DH, -1)
            w1bd = _block_diag(jnp.stack([w1k, w1k, w1v, w1v], axis=1).astype(bf16))
            w2k, w2v = cmp_w2[j, 0], cmp_w2[j, 1]
            zb = jnp.zeros_like(w2k)
            w2p = jnp.concatenate([
                jnp.concatenate([w2k, zb, zb, zb], axis=1),
                jnp.concatenate([zb, zb, w2k, zb], axis=1),
                jnp.concatenate([zb, w2v, zb, zb], axis=1),
                jnp.concatenate([zb, zb, zb, w2v], axis=1)], axis=0).astype(bf16)
            pe_row = jnp.concatenate([cmp_pe[j, 0], cmp_pe[j, 0], cmp_pe[j, 1], cmp_pe[j, 1]], axis=1)

            kcv_p = _compress(z3, c_cmp // KV_LANES, pe_row, w1bd, w2p, Wp)
            oc_p, sel_p = _nsa_cmp_prompt(z3, kcv_p, bias_cmp_p, Wp, n_blk_p)
            os_p = _nsa_sel(z3, c_sel // LANES, tbl_sel, sel_p, expand_pt)
            ow_p = _nsa_win(z3, c_win // LANES, tbl_win)

            kcv_s = _compress_paged(page_table, cmp_t, j, pe_row, w1bd, w2p, Ws)
            q_s5 = z_s[:, :nq_cols].reshape(DB, TS, NSA_G, NSA_HPG, NSA_DH)
            q_tj = q_s5.transpose(0, 2, 1, 3, 4).reshape(DB, NSA_G, TS * NSA_HPG, NSA_DH)
            oc_s, sel_s, mexp_s = _nsa_cmp_sample(q_tj, kcv_s, bias_cmp_s, expand_s, Ws, n_blk_s, past)
            oc_s = oc_s.reshape(DB, NSA_G, TS, NSA_HPG, NSA_DH).transpose(0, 2, 1, 3, 4)
            q_jgt = q_s5.transpose(0, 3, 2, 1, 4) * NSA_SCALE
            lane_g = (jnp.arange(KV_LANES) // NSA_DH)[None, None, None, None, :] == jnp.arange(NSA_G)[None, None, :, None, None]
            q_aug = jnp.where(lane_g, jnp.tile(q_jgt, (1, 1, 1, 1, KV_LANES // NSA_DH)), 0.0)
            q_aug = q_aug.reshape(DB, NSA_HPG * NSA_G * TS, KV_LANES)
            new_sel = sel_rows[Mp:].reshape(DB, TS, KV_LANES)
            new_win = win_rows[Mp:].reshape(DB, TS, KV_LANES)
            cur_blk = past // SEL_BLOCK
            member = sel_s[:, :, cur_blk].reshape(DB, 1, NSA_G * TS)
            member = jnp.broadcast_to(member, (DB, NSA_HPG, NSA_G * TS)).reshape(DB, -1, 1)
            xb_sel = jnp.where(member > 0.5, xb_nsa[None], NEG)
            o_sel = _paged_attn(page_table, sel_t, j, q_aug, pad_rows(new_sel, 8), xb_sel, True,
                                bias=bias_sel_s, mask=mexp_s)
            o_win = _paged_attn(win_pages, win_t, j, q_aug, pad_rows(new_win, 8), xb_nsa[None], True,
                                bias=bias_win_s)

            def unpack(o):
                o = o.reshape(DB, NSA_HPG, NSA_G, TS, 2, NSA_G, NSA_DH)[:, :, :, :, 1]
                o = jnp.stack([o[:, :, g, :, g] for g in range(NSA_G)], axis=2)
                return o.transpose(0, 3, 2, 1, 4)

            os_s, ow_s = unpack(o_sel), unpack(o_win)

            def cat(p_, s_):
                return jnp.concatenate([p_.reshape(Mp, NSA_HEADS, NSA_DH), s_.reshape(Ms, NSA_HEADS, NSA_DH)], axis=0)

            o = (gates[..., 0:1] * cat(oc_p, oc_s) + gates[..., 1:2] * cat(os_p, os_s)
                 + gates[..., 2:3] * cat(ow_p, ow_s)).reshape(-1, nq_cols)
            h = _mm(o, od_w_out[j].astype(bf16), resid=h)

            wrows = min(WINDOW, T)
            outs["cmp_p"].append(cmp_rows[:Mp].reshape(B, T, 2, NSA_G, NSA_DH))
            outs["cmp_s"].append(cmp_rows[Mp:].reshape(DB, TS, 2, NSA_G, NSA_DH))
            outs["sel_p"].append(sel_rows[:Mp].reshape(B, T, 2, NSA_G, NSA_DH))
            outs["sel_s"].append(sel_rows[Mp:].reshape(DB, TS, 2, NSA_G, NSA_DH))
            outs["win_p"].append(win_rows[:Mp].reshape(B, T, 2, NSA_G, NSA_DH)[:, T - wrows:])
            win_cat = jnp.concatenate([state_nsa_win[j], win_rows[Mp:].reshape(DB, TS, 2, NSA_G, NSA_DH)], axis=1)
            outs["win_s"].append(win_cat[:, TS:])

        up = _mm(h, ffn_w_up[layer].astype(bf16), g=norm_ffn[layer])
        w_down = ffn_w_down[layer].astype(bf16)
        a_s = up[Mp:, :D_FF].reshape(DB, TS, D_FF)
        a_ext_s = jnp.concatenate([state_ffn_conv[layer], a_s], axis=1)
        h_p = _ffn_down_prompt(up, ffn_conv_w[layer], ffn_conv_b[layer], w_down, h, Mp, T)
        h_s = _ffn_down_rows(up, a_ext_s[:, 1:1 + TS].reshape(Ms, D_FF), a_ext_s[:, 0:TS].reshape(Ms, D_FF),
                             ffn_conv_w[layer], ffn_conv_b[layer], w_down, h, Mp)
        h = jnp.concatenate([h_p, h_s], axis=0)
        last2 = (np.arange(B)[:, None] * T + np.arange(T - 2, T)[None, :]).reshape(-1)
        outs["conv_p"].append(up[last2][:, :D_FF].reshape(B, 2, D_FF))
        outs["conv_s"].append(a_ext_s[:, TS:])

    y = _rmsnorm(h, final_norm)
    st = lambda k: jnp.stack(outs[k])
    return (y[:Mp].reshape(B, T, D), y[Mp:].reshape(DB, TS, D),
            st("ckv_p"), st("ckv_s"), st("kr_p"), st("kr_s"),
            st("cmp_p"), st("cmp_s"), st("sel_p"), st("sel_s"),
            st("win_p"), st("win_s"), st("ret_p"), st("ret_s"),
            st("conv_p"), st("conv_s"))
```

```python
import functools
import math

import numpy as np
import jax
import jax.numpy as jnp
from jax import lax
from jax.experimental import pallas as pl
from jax.experimental.pallas import tpu as pltpu

f32 = jnp.float32
bf16 = jnp.bfloat16

PAGE = 128
MLA_HEADS = 8
MLA_Q_RANK = 384
MLA_KV_RANK = 256
MLA_NOPE = 64
MLA_ROPE = 32
MLA_V = 64
MLA_SCALE = (MLA_NOPE + MLA_ROPE) ** -0.5
RET_HEADS = 4
RET_D = 128
NSA_HEADS = 16
NSA_G = 2
NSA_HPG = 8
NSA_DH = 64
NSA_SCALE = NSA_DH ** -0.5
CMP_BLOCK = 32
SEL_BLOCK = 64
N_SEL = 16
WINDOW = 512
D_FF = 2816
N_BUCKETS = 32
MAX_DISTANCE = 128
ROPE_BASE = 10000.0
EPS = 1e-6
NEG = -1e30
FORCE = 1e4
KV_LANES = 2 * NSA_G * NSA_DH
LANES = 128
FLASH_TK = 1024
XS_PITCH = 72

VMEM_LIMIT = 56 * 1024 * 1024


def _cparams(sem):
    return pltpu.CompilerParams(dimension_semantics=sem, vmem_limit_bytes=VMEM_LIMIT)


def _dot(a, b):
    return jnp.dot(a, b, preferred_element_type=f32)


def _dot_nt(a, b):
    return lax.dot_general(a, b, (((1,), (1,)), ((), ())), preferred_element_type=f32)


def _row_tile(m, cap):
    for t in (512, 256, 128, 64, 32, 16, 8):
        if t <= cap and m % t == 0:
            return t
    return m


def _pick_tile(n, cap):
    best = None
    for t in range(128, min(n, cap) + 1, 128):
        if n % t == 0:
            best = t
    return best if best is not None else n


def _mm_kernel(*refs, norm, resid):
    it = iter(refs)
    x_ref = next(it)
    g_ref = next(it) if norm else None
    w_ref = next(it)
    r_ref = next(it) if resid else None
    o_ref = next(it)
    xn_ref = next(it)

    @pl.when(pl.program_id(1) == 0)
    def _():
        x = x_ref[...]
        if norm:
            x = x * lax.rsqrt(jnp.mean(x * x, axis=-1, keepdims=True) + EPS) * g_ref[...]
        xn_ref[...] = x.astype(bf16)

    acc = _dot(xn_ref[...], w_ref[...])
    if resid:
        acc = acc + r_ref[...]
    o_ref[...] = acc


def _mm(x, w, g=None, resid=None, tm=512):
    M = x.shape[0]
    K, N = w.shape
    tm = _row_tile(M, tm)
    assert K == x.shape[1] or K % 128 == 0
    tn = _pick_tile(N, 1536)
    norm = g is not None
    has_r = resid is not None
    in_specs = [pl.BlockSpec((tm, K), lambda i, j: (i, 0))]
    args = [x]
    if norm:
        in_specs.append(pl.BlockSpec((1, K), lambda i, j: (0, 0)))
        args.append(g.reshape(1, K).astype(f32))
    in_specs.append(pl.BlockSpec((K, tn), lambda i, j: (0, j)))
    args.append(w)
    if has_r:
        in_specs.append(pl.BlockSpec((tm, tn), lambda i, j: (i, j)))
        args.append(resid)
    return pl.pallas_call(
        functools.partial(_mm_kernel, norm=norm, resid=has_r),
        out_shape=jax.ShapeDtypeStruct((M, N), f32),
        grid=(M // tm, N // tn),
        in_specs=in_specs,
        out_specs=pl.BlockSpec((tm, tn), lambda i, j: (i, j)),
        scratch_shapes=[pltpu.VMEM((tm, K), bf16)],
        compiler_params=_cparams(("parallel", "arbitrary")),
        name="mm",
    )(*args)


def _rmsnorm_kernel(x_ref, g_ref, o_ref):
    x = x_ref[...]
    o_ref[...] = x * lax.rsqrt(jnp.mean(x * x, axis=-1, keepdims=True) + EPS) * g_ref[...]


def _rmsnorm(x, g, tm=512):
    M, D = x.shape
    tm = _row_tile(M, tm)
    return pl.pallas_call(
        _rmsnorm_kernel,
        out_shape=jax.ShapeDtypeStruct((M, D), f32),
        grid=(M // tm,),
        in_specs=[pl.BlockSpec((tm, D), lambda i: (i, 0)), pl.BlockSpec((1, D), lambda i: (0, 0))],
        out_specs=pl.BlockSpec((tm, D), lambda i: (i, 0)),
        compiler_params=_cparams(("parallel",)),
        name="rmsnorm",
    )(x, g.reshape(1, D))


EV_CKV = MLA_Q_RANK
EV_KR = MLA_Q_RANK + MLA_KV_RANK
EV_RET = 1024
EV_N = EV_RET + 4 * RET_HEADS * RET_D


def _even_post_kernel(z_ref, q_ref, g_ref, c32_ref, s32_ref, c64_ref, s64_ref,
                      ckv_ref, kr_ref, qp_ref, rq_ref, rk_ref):
    zc = z_ref[:, EV_CKV:EV_KR]
    ckv_ref[...] = zc * lax.rsqrt(jnp.mean(zc * zc, axis=-1, keepdims=True) + EPS) * g_ref[...]

    c32, s32 = c32_ref[...], s32_ref[...]
    lane = lax.broadcasted_iota(jnp.int32, c32.shape, 1)
    first_half = lax.rem(lane, MLA_ROPE) < MLA_ROPE // 2

    def rope32(x):
        swapped = jnp.where(first_half, pltpu.roll(x, LANES - MLA_ROPE // 2, 1), pltpu.roll(x, MLA_ROPE // 2, 1))
        return x * c32 + swapped * s32

    kr_ref[...] = rope32(z_ref[:, EV_KR:EV_KR + LANES])
    q0 = MLA_HEADS * MLA_NOPE
    for t in range(MLA_HEADS * MLA_ROPE // LANES):
        qp_ref[:, t * LANES:(t + 1) * LANES] = rope32(q_ref[:, q0 + t * LANES:q0 + (t + 1) * LANES]) * MLA_SCALE

    c64, s64 = c64_ref[...], s64_ref[...]
    nr = RET_HEADS * RET_D
    for h in range(RET_HEADS):
        xq = z_ref[:, EV_RET + h * RET_D:EV_RET + (h + 1) * RET_D]
        rq_ref[:, h * RET_D:(h + 1) * RET_D] = xq * c64 + pltpu.roll(xq, RET_D // 2, 1) * s64
        xk = z_ref[:, EV_RET + nr + h * RET_D:EV_RET + nr + (h + 1) * RET_D]
        rk_ref[:, h * RET_D:(h + 1) * RET_D] = (xk * c64 + pltpu.roll(xk, RET_D // 2, 1) * s64) * RET_D ** -0.5


def _even_post(z, q, kv_gain, c32, s32, c64, s64, tm=256):
    M = z.shape[0]
    tm = _row_tile(M, tm)
    nr = RET_HEADS * RET_D
    row = lambda i: (i, 0)
    widths = (MLA_KV_RANK, LANES, MLA_HEADS * MLA_ROPE, nr, nr)
    return pl.pallas_call(
        _even_post_kernel,
        out_shape=tuple(jax.ShapeDtypeStruct((M, w), f32) for w in widths),
        grid=(M // tm,),
        in_specs=[pl.BlockSpec((tm, EV_N), row), pl.BlockSpec((tm, q.shape[1]), row),
                  pl.BlockSpec((1, MLA_KV_RANK), lambda i: (0, 0))]
        + [pl.BlockSpec((tm, LANES), row)] * 4,
        out_specs=tuple(pl.BlockSpec((tm, w), row) for w in widths),
        compiler_params=_cparams(("parallel",)),
        name="even_post",
    )(z, q, kv_gain.reshape(1, MLA_KV_RANK), c32, s32, c64, s64)


def _ffn_gate_down(a, a1, a2, u, cw_ref, cb_ref, w_ref, r_ref, o_ref):
    cw = cw_ref[...]
    c = cb_ref[...] + a2 * cw[0:1] + a1 * cw[1:2] + a * cw[2:3]
    gate = c * jax.nn.sigmoid(c) * u
    o_ref[...] = _dot(gate.astype(bf16), w_ref[...]) + r_ref[...]


def _ffn_down_prompt_kernel(a_ref, prev_ref, u_ref, cw_ref, cb_ref, w_ref, r_ref, o_ref, *, tiles_per_seq):
    a = a_ref[...]
    keep = jnp.where(lax.rem(pl.program_id(0), tiles_per_seq) == 0, 0.0, 1.0)
    p7 = prev_ref[7:8, :] * keep
    p6 = prev_ref[6:7, :] * keep
    row = lax.broadcasted_iota(jnp.int32, a.shape, 0)
    a1 = jnp.where(row == 0, p7, pltpu.roll(a, 1, 0))
    a2 = jnp.where(row == 0, p6, jnp.where(row == 1, p7, pltpu.roll(a, 2, 0)))
    _ffn_gate_down(a, a1, a2, u_ref[...], cw_ref, cb_ref, w_ref, r_ref, o_ref)


def _ffn_down_prompt(up, conv_w, conv_b, w_down, resid, Mp, T, tm=256):
    D = w_down.shape[1]
    tm = _row_tile(T, tm)
    row = lambda i: (i, 0)
    return pl.pallas_call(
        functools.partial(_ffn_down_prompt_kernel, tiles_per_seq=T // tm),
        out_shape=jax.ShapeDtypeStruct((Mp, D), f32),
        grid=(Mp // tm,),
        in_specs=[
            pl.BlockSpec((tm, D_FF), row),
            pl.BlockSpec((8, D_FF), lambda i: (jnp.maximum(i * (tm // 8) - 1, 0), 0)),
            pl.BlockSpec((tm, D_FF), lambda i: (i, 1)),
            pl.BlockSpec((3, D_FF), lambda i: (0, 0)),
            pl.BlockSpec((1, D_FF), lambda i: (0, 0)),
            pl.BlockSpec((D_FF, D), lambda i: (0, 0)),
            pl.BlockSpec((tm, D), row),
        ],
        out_specs=pl.BlockSpec((tm, D), row),
        compiler_params=_cparams(("parallel",)),
        name="ffn_down_prompt",
    )(up, up, up, conv_w, conv_b.reshape(1, D_FF), w_down, resid)


def _ffn_down_rows_kernel(a_ref, a1_ref, a2_ref, u_ref, cw_ref, cb_ref, w_ref, r_ref, o_ref):
    _ffn_gate_down(a_ref[...], a1_ref[...], a2_ref[...], u_ref[...], cw_ref, cb_ref, w_ref, r_ref, o_ref)


def _ffn_down_rows(up, a1, a2, conv_w, conv_b, w_down, resid, row0, tm=256):
    Ms = a1.shape[0]
    D = w_down.shape[1]
    tm = _row_tile(math.gcd(Ms, row0) if row0 else Ms, tm)
    off = row0 // tm
    row = lambda i: (i, 0)
    return pl.pallas_call(
        _ffn_down_rows_kernel,
        out_shape=jax.ShapeDtypeStruct((Ms, D), f32),
        grid=(Ms // tm,),
        in_specs=[
            pl.BlockSpec((tm, D_FF), lambda i: (i + off, 0)),
            pl.BlockSpec((tm, D_FF), row),
            pl.BlockSpec((tm, D_FF), row),
            pl.BlockSpec((tm, D_FF), lambda i: (i + off, 1)),
            pl.BlockSpec((3, D_FF), lambda i: (0, 0)),
            pl.BlockSpec((1, D_FF), lambda i: (0, 0)),
            pl.BlockSpec((D_FF, D), lambda i: (0, 0)),
            pl.BlockSpec((tm, D), lambda i: (i + off, 0)),
        ],
        out_specs=pl.BlockSpec((tm, D), row),
        compiler_params=_cparams(("parallel",)),
        name="ffn_down_rows",
    )(up, a1, a2, up, conv_w, conv_b.reshape(1, D_FF), w_down, resid)


def _retention_kernel(*refs, gated):
    if gated:
        q_ref, k_ref, v_ref, g_ref, dm_ref, qd_ref, kd_ref, cd_ref, s0_ref, o_ref, so_ref, s_scr = refs
    else:
        q_ref, k_ref, v_ref, dm_ref, qd_ref, kd_ref, cd_ref, s0_ref, o_ref, so_ref, s_scr = refs
    c = pl.program_id(1)

    @pl.when(c == 0)
    def _():
        s_scr[...] = s0_ref[0]

    for h in range(RET_HEADS):
        lanes = slice(h * RET_D, (h + 1) * RET_D)
        q = q_ref[0, :, lanes]
        k = k_ref[0, :, lanes]
        v = v_ref[0, :, lanes].astype(bf16)
        S = s_scr[h]
        inner = _dot_nt(q.astype(bf16), k.astype(bf16)) * dm_ref[h]
        o = _dot(inner.astype(bf16), v) + _dot((q * qd_ref[h]).astype(bf16), S.astype(bf16))
        if gated:
            gate = g_ref[0, :, lanes]
            o = o * lax.rsqrt(jnp.mean(o * o, axis=-1, keepdims=True) + EPS) * (gate * jax.nn.sigmoid(gate))
        o_ref[0, :, lanes] = o
        kd = (k * kd_ref[h]).T
        s_new = S * cd_ref[h, 0:1, :] + _dot(kd.astype(bf16), v)
        s_scr[h] = s_new
        so_ref[0, h] = s_new


def _retention(q, k, v, s0, length, vcol=0, gate=None, gcol=0):
    Bn, Tn, _ = q.shape
    C = RET_D
    nc = Tn // C
    H = RET_HEADS
    log_g = jnp.log(1.0 - 2.0 ** (-5.0 - jnp.arange(H, dtype=f32)))
    idx = jnp.arange(C, dtype=f32)
    diff = idx[:, None] - idx[None, :]
    dmask = jnp.where(diff >= 0, jnp.exp(jnp.maximum(diff, 0.0) * log_g[:, None, None]), 0.0)
    q_dec = jnp.exp((idx + 1.0) * log_g[:, None])
    k_dec = jnp.where(idx < length, jnp.exp(jnp.maximum(length - 1.0 - idx, 0.0) * log_g[:, None]), 0.0)
    c_dec = jnp.exp(length * log_g)
    qd = jnp.broadcast_to(q_dec[:, :, None], (H, C, C))
    kd = jnp.broadcast_to(k_dec[:, :, None], (H, C, C))
    cd = jnp.broadcast_to(c_dec[:, None, None], (H, 8, C))
    qkv_spec = pl.BlockSpec((1, C, H * C), lambda b, c: (b, c, 0))
    tbl_spec = pl.BlockSpec((H, C, C), lambda b, c: (0, 0, 0))
    st_spec = pl.BlockSpec((1, H, C, C), lambda b, c: (b, 0, 0, 0))
    gated = gate is not None
    in_specs = [qkv_spec, qkv_spec, pl.BlockSpec((1, C, H * C), lambda b, c: (b, c, vcol))]
    args = [q, k, v]
    if gated:
        in_specs.append(pl.BlockSpec((1, C, H * C), lambda b, c: (b, c, gcol)))
        args.append(gate)
    in_specs += [tbl_spec, tbl_spec, tbl_spec, pl.BlockSpec((H, 8, C), lambda b, c: (0, 0, 0)), st_spec]
    return pl.pallas_call(
        functools.partial(_retention_kernel, gated=gated),
        out_shape=(jax.ShapeDtypeStruct((Bn, Tn, H * C), f32), jax.ShapeDtypeStruct((Bn, H, C, C), f32)),
        grid=(Bn, nc),
        in_specs=in_specs,
        out_specs=(qkv_spec, st_spec),
        scratch_shapes=[pltpu.VMEM((H, C, C), f32)],
        compiler_params=_cparams(("parallel", "arbitrary")),
        name="retention",
    )(*args, dmask, qd, kd, cd, s0)


def _pairs(nq, jlo, jhi):
    ii, jj, fl = [], [], []
    for i in range(nq):
        lo, hi = jlo(i), jhi(i)
        for j in range(lo, hi + 1):
            ii.append(i)
            jj.append(j)
            fl.append((1 if j == lo else 0) | (2 if j == hi else 0))
    return (jnp.asarray(np.array(ii, np.int32)), jnp.asarray(np.array(jj, np.int32)),
            jnp.asarray(np.array(fl, np.int32)))


def _add_bias_tiles(s, tbl_ref, i, tile0, nd):
    parts = []
    for c in range(s.shape[1] // LANES):
        d = i - (tile0 + c)
        idx = jnp.where(d < 0, nd, jnp.minimum(d, nd - 1))
        parts.append(s[:, c * LANES:(c + 1) * LANES] + tbl_ref[0, idx])
    return jnp.concatenate(parts, axis=1)


def _flash_init(m_s, l_s, acc_s):
    m_s[...] = jnp.full(m_s.shape, NEG, f32)
    l_s[...] = jnp.zeros(l_s.shape, f32)
    acc_s[...] = jnp.zeros(acc_s.shape, f32)


def _flash_update(s, v, m_s, l_s, acc_s):
    m_prev = m_s[...]
    m_new = jnp.maximum(m_prev, jnp.max(s, axis=-1, keepdims=True))
    alpha = jnp.exp(m_prev - m_new)
    pr = jnp.exp(s - m_new)
    l_s[...] = alpha * l_s[...] + jnp.sum(pr, axis=-1, keepdims=True)
    acc_s[...] = alpha * acc_s[...] + _dot(pr.astype(bf16), v)
    m_s[...] = m_new


def _mla_prompt_kernel(ii_ref, jj_ref, fl_ref, qn_ref, qp_ref, wuk_ref, ckv_ref, kr_ref, tbl_ref, wuv_ref, o_ref,
                       qa_s, qp_s, m_s, l_s, acc_s, *, tq, tk):
    p = pl.program_id(1)
    i = ii_ref[p]
    j = jj_ref[p]
    fl = fl_ref[p]
    H = MLA_HEADS

    @pl.when((fl & 1) == 1)
    def _():
        qp_s[...] = jnp.zeros(qp_s.shape, bf16)
        for h in range(H):
            qh = qn_ref[0, :, h * MLA_NOPE:(h + 1) * MLA_NOPE].astype(bf16)
            qa = _dot(qh, wuk_ref[h]) * MLA_SCALE
            qa_s[h * tq:(h + 1) * tq, :] = qa.astype(bf16)
            qp_s[h * tq:(h + 1) * tq, 0:MLA_ROPE] = qp_ref[0, :, h * MLA_ROPE:(h + 1) * MLA_ROPE].astype(bf16)
        _flash_init(m_s, l_s, acc_s)

    kv = ckv_ref[0].astype(bf16)
    kr = kr_ref[0].astype(bf16)
    s = _dot_nt(qa_s[...], kv) + _dot_nt(qp_s[...], kr)
    s = _add_bias_tiles(s, tbl_ref, i, j * (tk // LANES), tbl_ref.shape[1] - 1)
    _flash_update(s, kv, m_s, l_s, acc_s)

    @pl.when((fl & 2) == 2)
    def _():
        o = acc_s[...] / l_s[...]
        for h in range(H):
            o_ref[0, :, h * MLA_V:(h + 1) * MLA_V] = _dot(o[h * tq:(h + 1) * tq].astype(bf16), wuv_ref[h])


def _mla_prompt(qn, qp, ckv, kr, wuk_t, wuv, tk=FLASH_TK):
    B, T, _ = qn.shape
    tq = LANES
    tk = min(tk, T)
    H = MLA_HEADS
    R = H * tq
    ii, jj, fl = _pairs(T // tq, lambda i: 0, lambda i: (i * tq + tq - 1) // tk)
    npairs = int(ii.shape[0])
    r = np.arange(tq)[:, None]
    c = np.arange(LANES)[None, :]
    causal = np.where(c <= r, 0.0, NEG).astype(np.float32)
    tbl = np.stack([np.tile(causal, (H, 1)), np.zeros((R, LANES), np.float32), np.full((R, LANES), NEG, np.float32)])
    grid_spec = pltpu.PrefetchScalarGridSpec(
        num_scalar_prefetch=3,
        grid=(B, npairs),
        in_specs=[
            pl.BlockSpec((1, tq, H * MLA_NOPE), lambda b, p, ii, jj, fl: (b, ii[p], 0)),
            pl.BlockSpec((1, tq, H * MLA_ROPE), lambda b, p, ii, jj, fl: (b, ii[p], 0)),
            pl.BlockSpec((H, MLA_NOPE, MLA_KV_RANK), lambda b, p, ii, jj, fl: (0, 0, 0)),
            pl.BlockSpec((1, tk, MLA_KV_RANK), lambda b, p, ii, jj, fl: (b, jj[p], 0)),
            pl.BlockSpec((1, tk, LANES), lambda b, p, ii, jj, fl: (b, jj[p], 0)),
            pl.BlockSpec((1, 3, R, LANES), lambda b, p, ii, jj, fl: (0, 0, 0, 0)),
            pl.BlockSpec((H, MLA_KV_RANK, MLA_V), lambda b, p, ii, jj, fl: (0, 0, 0)),
        ],
        out_specs=pl.BlockSpec((1, tq, H * MLA_V), lambda b, p, ii, jj, fl: (b, ii[p], 0)),
        scratch_shapes=[
            pltpu.VMEM((R, MLA_KV_RANK), bf16),
            pltpu.VMEM((R, LANES), bf16),
            pltpu.VMEM((R, 1), f32),
            pltpu.VMEM((R, 1), f32),
            pltpu.VMEM((R, MLA_KV_RANK), f32),
        ],
    )
    return pl.pallas_call(
        functools.partial(_mla_prompt_kernel, tq=tq, tk=tk),
        out_shape=jax.ShapeDtypeStruct((B, T, H * MLA_V), f32),
        grid_spec=grid_spec,
        compiler_params=_cparams(("parallel", "arbitrary")),
        name="mla_prompt",
    )(ii, jj, fl, qn, qp, wuk_t, ckv, kr, jnp.asarray(tbl)[None], wuv)


def _page_copies(pt_ref, cache, buf, sem, layer, npg, seq, slot, transposed):
    cps = []
    for p in range(npg):
        src = cache.at[layer, pt_ref[seq * npg + p]]
        if len(buf.shape) == 4:
            dst = buf.at[slot, p]
        elif transposed:
            pw = cache.shape[3]
            dst = buf.at[slot, :, pl.ds(p * pw, pw)]
        else:
            pw = cache.shape[2]
            dst = buf.at[slot, pl.ds(p * pw, pw)]
        cps.append(pltpu.make_async_copy(src, dst, sem.at[slot]))
    return cps


def _gather_step(pt_ref, caches, bufs, sems, transposed, layer, npg):
    b = pl.program_id(0)
    nb = pl.num_programs(0)

    def copies(seq, slot):
        out = []
        for cache, buf, sem, tr in zip(caches, bufs, sems, transposed):
            out += _page_copies(pt_ref, cache, buf, sem, layer, npg, seq, slot, tr)
        return out

    @pl.when(b == 0)
    def _():
        for cp in copies(0, 0):
            cp.start()

    @pl.when(b + 1 < nb)
    def _():
        for cp in copies(b + 1, lax.rem(b + 1, 2)):
            cp.start()

    slot = lax.rem(b, 2)
    for cp in copies(b, slot):
        cp.wait()
    return slot


def _paged_attn_kernel(pt_ref, *refs, layer, npg, main_t, has2, has_bias, has_mask):
    it = iter(refs)
    q_ref = next(it)
    q2_ref = next(it) if has2 else None
    bias_ref = next(it) if has_bias else None
    mask_ref = next(it) if has_mask else None
    xk_ref = next(it)
    xk2_ref = next(it) if has2 else None
    xb_ref = next(it)
    cache = next(it)
    cache2 = next(it) if has2 else None
    o_ref = next(it)
    buf = next(it)
    buf2 = next(it) if has2 else None
    sem = next(it)
    sem2 = next(it) if has2 else None

    caches, bufs, sems, trs = [cache], [buf], [sem], [main_t]
    if has2:
        caches.append(cache2)
        bufs.append(buf2)
        sems.append(sem2)
        trs.append(True)
    slot = _gather_step(pt_ref, caches, bufs, sems, trs, layer, npg)

    q = q_ref[0].astype(bf16)
    pages = buf[slot].astype(bf16)
    xk = xk_ref[0].astype(bf16)
    s = _dot(q, pages) if main_t else _dot_nt(q, pages)
    sx = _dot_nt(q, xk)
    if has2:
        q2 = q2_ref[0].astype(bf16)
        s = s + _dot(q2, buf2[slot].astype(bf16))
        sx = sx + _dot_nt(q2, xk2_ref[0].astype(bf16))
    if has_bias:
        s = s + bias_ref[...]
    if has_mask:
        reps = s.shape[0] // mask_ref.shape[1]
        s = s + jnp.concatenate([mask_ref[0]] * reps, axis=0)
    sx = sx + xb_ref[0]
    m = jnp.maximum(jnp.max(s, axis=-1, keepdims=True), jnp.max(sx, axis=-1, keepdims=True))
    pr = jnp.exp(s - m)
    px = jnp.exp(sx - m)
    l = jnp.sum(pr, axis=-1, keepdims=True) + jnp.sum(px, axis=-1, keepdims=True)
    pb = pr.astype(bf16)
    o = (_dot_nt(pb, pages) if main_t else _dot(pb, pages)) + _dot(px.astype(bf16), xk)
    o_ref[0] = jnp.where(m > 0.5 * NEG, o / l, 0.0)


def _paged_attn(page_table, cache, layer, q, xk, xb, main_t, q2=None, cache2=None, xk2=None, bias=None, mask=None):
    DB, R, DL = q.shape
    npg = page_table.shape[1]
    pw = cache.shape[3] if main_t else cache.shape[2]
    K = npg * pw
    has2 = q2 is not None
    seq = lambda b, pt: (b, 0, 0)
    in_specs = [pl.BlockSpec((1, R, DL), seq)]
    args = [q]
    if has2:
        d2 = q2.shape[-1]
        in_specs.append(pl.BlockSpec((1, R, d2), seq))
        args.append(q2)
    if bias is not None:
        in_specs.append(pl.BlockSpec((R, K), lambda b, pt: (0, 0)))
        args.append(bias)
    if mask is not None:
        in_specs.append(pl.BlockSpec((1, 8, K), seq))
        args.append(mask)
    in_specs.append(pl.BlockSpec((1, 8, DL), seq))
    args.append(xk)
    if has2:
        in_specs.append(pl.BlockSpec((1, 8, d2), seq))
        args.append(xk2)
    if xb.shape[0] == 1:
        in_specs.append(pl.BlockSpec((1, R, 8), lambda b, pt: (0, 0, 0)))
    else:
        in_specs.append(pl.BlockSpec((1, R, 8), seq))
    args.append(xb)
    in_specs.append(pl.BlockSpec(memory_space=pl.ANY))
    args.append(cache)
    scratch = [pltpu.VMEM((2, DL, K) if main_t else (2, K, DL), f32)]
    if has2:
        in_specs.append(pl.BlockSpec(memory_space=pl.ANY))
        args.append(cache2)
        scratch.append(pltpu.VMEM((2, d2, K), f32))
    scratch.append(pltpu.SemaphoreType.DMA((2,)))
    if has2:
        scratch.append(pltpu.SemaphoreType.DMA((2,)))
    grid_spec = pltpu.PrefetchScalarGridSpec(
        num_scalar_prefetch=1,
        grid=(DB,),
        in_specs=in_specs,
        out_specs=pl.BlockSpec((1, R, DL), seq),
        scratch_shapes=scratch,
    )
    return pl.pallas_call(
        functools.partial(_paged_attn_kernel, layer=layer, npg=npg, main_t=main_t, has2=has2,
                          has_bias=bias is not None, has_mask=mask is not None),
        out_shape=jax.ShapeDtypeStruct((DB, R, DL), f32),
        grid_spec=grid_spec,
        compiler_params=_cparams(("arbitrary",)),
        name="paged_attn",
    )(page_table.reshape(-1), *args)


def _compress_compute(load_rows, pe_ref, w1_ref, w2_ref, o_ref, nblk, W):
    half = nblk // 2
    acc = jnp.zeros((nblk, KV_LANES), f32)
    for r in range(CMP_BLOCK):
        x = jnp.concatenate([load_rows(r, half), load_rows(CMP_BLOCK + r, half)], axis=0) + pe_ref[r:r + 1, :]
        acc = acc + _dot(x.astype(bf16), w1_ref[r])
    hdn = acc * jax.nn.sigmoid(acc)
    out = _dot(hdn.astype(bf16), w2_ref[...])
    o_ref[0, 0:half, :] = out[:half]
    o_ref[0, W:W + half, :] = out[half:]
    if half < W:
        zeros = jnp.zeros((W - half, KV_LANES), f32)
        o_ref[0, half:W, :] = zeros
        o_ref[0, W + half:2 * W, :] = zeros


def _compress_kernel(xlo_ref, xhi_ref, pe_ref, w1_ref, w2_ref, o_ref, *, nblk, W):
    def load(r, n):
        rows = pl.ds(r, n, stride=2 * CMP_BLOCK)
        return jnp.concatenate([xlo_ref[0, rows, :], xhi_ref[0, rows, :]], axis=1)

    _compress_compute(load, pe_ref, w1_ref, w2_ref, o_ref, nblk, W)


def _compress_weight_specs(idx):
    return [
        pl.BlockSpec((CMP_BLOCK, KV_LANES), idx(2)),
        pl.BlockSpec((CMP_BLOCK, KV_LANES, KV_LANES), idx(3)),
        pl.BlockSpec((KV_LANES, KV_LANES), idx(2)),
    ]


def _compress(z3, colblk, pe_row, w1bd, w2p, W):
    B, T, _ = z3.shape
    nblk = T // CMP_BLOCK
    return pl.pallas_call(
        functools.partial(_compress_kernel, nblk=nblk, W=W),
        out_shape=jax.ShapeDtypeStruct((B, 2 * W, KV_LANES), f32),
        grid=(B,),
        in_specs=[pl.BlockSpec((1, T, LANES), lambda b: (b, 0, 2 * colblk)),
                  pl.BlockSpec((1, T, LANES), lambda b: (b, 0, 2 * colblk + 1))]
        + _compress_weight_specs(lambda n: (lambda b: (0,) * n)),
        out_specs=pl.BlockSpec((1, 2 * W, KV_LANES), lambda b: (b, 0, 0)),
        compiler_params=_cparams(("parallel",)),
        name="compress",
    )(z3, z3, pe_row, w1bd, w2p)


def _compress_paged_kernel(pt_ref, pe_ref, w1_ref, w2_ref, cache, o_ref, buf, xs, sem, *, layer, npg, W):
    slot = _gather_step(pt_ref, [cache], [buf], [sem], [True], layer, npg)

    group = 2 * CMP_BLOCK

    def untranspose(p, carry):
        for sg in range(KV_LANES // LANES):
            t = buf[slot, p, sg * LANES:(sg + 1) * LANES, :].T
            for gi in range(PAGE // group):
                r0 = pl.multiple_of((p * (PAGE // group) + gi) * XS_PITCH, 8)
                xs[sg, pl.ds(r0, group), :] = t[gi * group:(gi + 1) * group]
        return carry

    lax.fori_loop(0, npg, untranspose, 0)

    def load(r, n):
        rows = pl.ds(r, n, stride=XS_PITCH)
        return jnp.concatenate([xs[sg, rows, :] for sg in range(KV_LANES // LANES)], axis=1)

    _compress_compute(load, pe_ref, w1_ref, w2_ref, o_ref, npg * PAGE // CMP_BLOCK, W)


def _compress_paged(page_table, cache_t, layer, pe_row, w1bd, w2p, W):
    DB, npg = page_table.shape
    grid_spec = pltpu.PrefetchScalarGridSpec(
        num_scalar_prefetch=1,
        grid=(DB,),
        in_specs=_compress_weight_specs(lambda n: (lambda b, pt: (0,) * n)) + [pl.BlockSpec(memory_space=pl.ANY)],
        out_specs=pl.BlockSpec((1, 2 * W, KV_LANES), lambda b, pt: (b, 0, 0)),
        scratch_shapes=[pltpu.VMEM((2, npg, KV_LANES, PAGE), f32),
                        pltpu.VMEM((KV_LANES // LANES, npg * PAGE // (2 * CMP_BLOCK) * XS_PITCH, LANES), f32),
                        pltpu.SemaphoreType.DMA((2,))],
    )
    return pl.pallas_call(
        functools.partial(_compress_paged_kernel, layer=layer, npg=npg, W=W),
        out_shape=jax.ShapeDtypeStruct((DB, 2 * W, KV_LANES), f32),
        grid_spec=grid_spec,
        compiler_params=_cparams(("arbitrary",)),
        name="compress_paged",
    )(page_table.reshape(-1), pe_row, w1bd, w2p, cache_t)


def _masked_softmax(s):
    valid = s > 0.5 * NEG
    m = jnp.max(s, axis=-1, keepdims=True)
    e = jnp.where(valid, jnp.exp(s - m), 0.0)
    l = jnp.sum(e, axis=-1, keepdims=True)
    return e * jnp.where(l > 0.0, 1.0 / l, 0.0)


def _select_blocks(imp, q_pos, n_blk):
    R, W = imp.shape
    blk = lax.broadcasted_iota(jnp.int32, (R, W), 1)
    cur = q_pos // SEL_BLOCK
    forced = (blk == 0) | (blk == cur) | (blk == cur - 1)
    valid = (blk * SEL_BLOCK <= q_pos) & (blk < n_blk)
    score = jnp.where(valid, jnp.where(forced, FORCE, imp), -1.0)
    score = jnp.where(blk < n_blk, score, -jnp.inf)
    rank = jnp.zeros((R, W), f32)
    for c in range(n_blk):
        col = score[:, c:c + 1]
        beats = (col > score) | ((col == score) & (blk > c))
        rank = rank + jnp.where(beats, 1.0, 0.0)
    return jnp.where((rank < float(min(N_SEL, n_blk))) & valid, 1.0, 0.0)


def _nsa_cmp_prompt_kernel(q_ref, kv_ref, bias_ref, oc_ref, sel_ref, *, tq, W, n_blk):
    i = pl.program_id(2)
    kv = kv_ref[0]
    k = kv[:, :NSA_DH].astype(bf16)
    v = kv[:, NSA_DH:].astype(bf16)
    imp = jnp.zeros((tq, 2 * W), f32)
    for h in range(NSA_HPG):
        qh = (q_ref[0, :, h * NSA_DH:(h + 1) * NSA_DH] * NSA_SCALE).astype(bf16)
        p = _masked_softmax(_dot_nt(qh, k) + bias_ref[h])
        oc_ref[0, :, h * NSA_DH:(h + 1) * NSA_DH] = _dot(p.astype(bf16), v)
        imp = imp + p
    imp = imp[:, :W] + imp[:, W:]
    q_pos = i * tq + lax.broadcasted_iota(jnp.int32, (tq, 1), 0)
    sel_ref[0, 0] = _select_blocks(imp, q_pos, n_blk)


def _nsa_cmp_prompt(z3, kcv, bias, W, n_blk, tq=LANES):
    B, T, _ = z3.shape
    return pl.pallas_call(
        functools.partial(_nsa_cmp_prompt_kernel, tq=tq, W=W, n_blk=n_blk),
        out_shape=(jax.ShapeDtypeStruct((B, T, NSA_HEADS * NSA_DH), f32),
                   jax.ShapeDtypeStruct((B, NSA_G, T, W), f32)),
        grid=(B, NSA_G, T // tq),
        in_specs=[
            pl.BlockSpec((1, tq, NSA_HPG * NSA_DH), lambda b, g, i: (b, i, g)),
            pl.BlockSpec((1, 2 * W, 2 * NSA_DH), lambda b, g, i: (b, 0, g)),
            pl.BlockSpec((NSA_HPG, tq, 2 * W), lambda b, g, i: (g, i, 0)),
        ],
        out_specs=(pl.BlockSpec((1, tq, NSA_HPG * NSA_DH), lambda b, g, i: (b, i, g)),
                   pl.BlockSpec((1, 1, tq, W), lambda b, g, i: (b, g, i, 0))),
        compiler_params=_cparams(("parallel", "parallel", "parallel")),
        name="nsa_cmp_prompt",
    )(z3, kcv, bias)


def _nsa_cmp_sample_kernel(q_ref, kv_ref, bias_ref, e_ref, oc_ref, sel_ref, mexp_ref, imp_s, *,
                           bt, ts, W, n_blk, past):
    for bb in range(bt):
        for g in range(NSA_G):
            k = kv_ref[bb, :, g * 2 * NSA_DH:g * 2 * NSA_DH + NSA_DH].astype(bf16)
            v = kv_ref[bb, :, g * 2 * NSA_DH + NSA_DH:(g + 1) * 2 * NSA_DH].astype(bf16)
            qg = (q_ref[bb, g] * NSA_SCALE).astype(bf16)
            p = _masked_softmax(_dot_nt(qg, k) + bias_ref[g])
            oc_ref[bb, g] = _dot(p.astype(bf16), v)
            imp = jnp.sum(p.reshape(ts, NSA_HPG, 2 * W), axis=1)
            r0 = (bb * NSA_G + g) * ts
            imp_s[r0:r0 + ts, :] = imp[:, :W] + imp[:, W:]
    rows = bt * NSA_G * ts
    q_pos = past + lax.rem(lax.broadcasted_iota(jnp.int32, (rows, 1), 0), ts)
    sel = _select_blocks(imp_s[...], q_pos, n_blk)
    sel_ref[...] = sel.reshape(bt, NSA_G * ts, W)
    hit = _dot(sel.astype(bf16), e_ref[...])
    mexp_ref[...] = jnp.where(hit > 0.5, 0.0, NEG).reshape(bt, NSA_G * ts, mexp_ref.shape[2])


def _nsa_cmp_sample(q, kcv, bias, expand, W, n_blk, past, bt=8):
    DB, _, R, _ = q.shape
    ts = R // NSA_HPG
    K = expand.shape[1]
    rows = NSA_G * ts
    return pl.pallas_call(
        functools.partial(_nsa_cmp_sample_kernel, bt=bt, ts=ts, W=W, n_blk=n_blk, past=past),
        out_shape=(jax.ShapeDtypeStruct((DB, NSA_G, R, NSA_DH), f32),
                   jax.ShapeDtypeStruct((DB, rows, W), f32),
                   jax.ShapeDtypeStruct((DB, rows, K), f32)),
        grid=(DB // bt,),
        in_specs=[
            pl.BlockSpec((bt, NSA_G, R, NSA_DH), lambda b: (b, 0, 0, 0)),
            pl.BlockSpec((bt, 2 * W, KV_LANES), lambda b: (b, 0, 0)),
            pl.BlockSpec((NSA_G, R, 2 * W), lambda b: (0, 0, 0)),
            pl.BlockSpec((W, K), lambda b: (0, 0)),
        ],
        out_specs=(pl.BlockSpec((bt, NSA_G, R, NSA_DH), lambda b: (b, 0, 0, 0)),
                   pl.BlockSpec((bt, rows, W), lambda b: (b, 0, 0)),
                   pl.BlockSpec((bt, rows, K), lambda b: (b, 0, 0))),
        scratch_shapes=[pltpu.VMEM((bt * rows, W), f32)],
        compiler_params=_cparams(("parallel",)),
        name="nsa_cmp_sample",
    )(q, kcv, bias, expand)


def _stack_heads(q_ref, q_s, tq):
    q_s[:, 0:2 * NSA_DH] = jnp.zeros((q_s.shape[0], 2 * NSA_DH), bf16)
    for h in range(NSA_HPG):
        q_s[h * tq:(h + 1) * tq, 0:NSA_DH] = (q_ref[0, :, h * NSA_DH:(h + 1) * NSA_DH] * NSA_SCALE).astype(bf16)


def _unstack_heads(o, o_ref, tq):
    for h in range(NSA_HPG):
        o_ref[0, :, h * NSA_DH:(h + 1) * NSA_DH] = o[h * tq:(h + 1) * tq, NSA_DH:]


def _nsa_sel_kernel(ii_ref, jj_ref, fl_ref, q_ref, kv_ref, tbl_ref, sel_ref, et_ref, o_ref,
                    q_s, m_s, l_s, acc_s, *, tq, tk):
    p = pl.program_id(2)
    i = ii_ref[p]
    j = jj_ref[p]
    fl = fl_ref[p]

    @pl.when((fl & 1) == 1)
    def _():
        _stack_heads(q_ref, q_s, tq)
        selneg = jnp.where(sel_ref[0, 0] > 0.5, 0.0, NEG).astype(bf16)
        for h in range(NSA_HPG):
            q_s[h * tq:(h + 1) * tq, 2 * NSA_DH:] = selneg
        _flash_init(m_s, l_s, acc_s)

    kvb = kv_ref[0].astype(bf16)
    s = _dot_nt(q_s[...], jnp.concatenate([kvb, et_ref[...]], axis=1))
    s = _add_bias_tiles(s, tbl_ref, i, j * (tk // LANES), tbl_ref.shape[1] - 1)
    _flash_update(s, kvb, m_s, l_s, acc_s)

    @pl.when((fl & 2) == 2)
    def _():
        _unstack_heads(acc_s[...] / l_s[...], o_ref, tq)


def _nsa_sel(z3, kvcol, tbl, sel, expand_t, tk=FLASH_TK):
    B, T, _ = z3.shape
    tq = LANES
    tk = min(tk, T)
    W = sel.shape[-1]
    R = NSA_HPG * tq
    ii, jj, fl = _pairs(T // tq, lambda i: 0, lambda i: (i * tq + tq - 1) // tk)
    npairs = int(ii.shape[0])
    grid_spec = pltpu.PrefetchScalarGridSpec(
        num_scalar_prefetch=3,
        grid=(B, NSA_G, npairs),
        in_specs=[
            pl.BlockSpec((1, tq, NSA_HPG * NSA_DH), lambda b, g, p, ii, jj, fl: (b, ii[p], g)),
            pl.BlockSpec((1, tk, 2 * NSA_DH), lambda b, g, p, ii, jj, fl: (b, jj[p], kvcol + g)),
            pl.BlockSpec((1, tbl.shape[1], R, LANES), lambda b, g, p, ii, jj, fl: (g, 0, 0, 0)),
            pl.BlockSpec((1, 1, tq, W), lambda b, g, p, ii, jj, fl: (b, g, ii[p], 0)),
            pl.BlockSpec((tk, W), lambda b, g, p, ii, jj, fl: (jj[p], 0)),
        ],
        out_specs=pl.BlockSpec((1, tq, NSA_HPG * NSA_DH), lambda b, g, p, ii, jj, fl: (b, ii[p], g)),
        scratch_shapes=[pltpu.VMEM((R, 2 * NSA_DH + W), bf16), pltpu.VMEM((R, 1), f32), pltpu.VMEM((R, 1), f32),
                        pltpu.VMEM((R, 2 * NSA_DH), f32)],
    )
    return pl.pallas_call(
        functools.partial(_nsa_sel_kernel, tq=tq, tk=tk),
        out_shape=jax.ShapeDtypeStruct((B, T, NSA_HEADS * NSA_DH), f32),
        grid_spec=grid_spec,
        compiler_params=_cparams(("parallel", "parallel", "arbitrary")),
        name="nsa_sel",
    )(ii, jj, fl, z3, z3, tbl, sel, expand_t)


def _nsa_win_kernel(q_ref, kv_ref, tbl_ref, o_ref, q_s, *, tq, nsub):
    i = pl.program_id(2)
    _stack_heads(q_ref, q_s, tq)
    tile0 = jnp.maximum(i - (nsub - 1), 0)
    kvb = kv_ref[0, pl.ds(pl.multiple_of(tile0 * LANES, LANES), nsub * LANES), :].astype(bf16)
    s = _add_bias_tiles(_dot_nt(q_s[...], kvb), tbl_ref, i, tile0, tbl_ref.shape[1] - 1)
    m = jnp.max(s, axis=-1, keepdims=True)
    pr = jnp.exp(s - m)
    l = jnp.sum(pr, axis=-1, keepdims=True)
    _unstack_heads(_dot(pr.astype(bf16), kvb) / l, o_ref, tq)


def _nsa_win(z3, kvcol, tbl):
    B, T, _ = z3.shape
    tq = LANES
    nsub = tbl.shape[1] - 1
    assert T >= nsub * LANES
    R = NSA_HPG * tq
    return pl.pallas_call(
        functools.partial(_nsa_win_kernel, tq=tq, nsub=nsub),
        out_shape=jax.ShapeDtypeStruct((B, T, NSA_HEADS * NSA_DH), f32),
        grid=(B, NSA_G, T // tq),
        in_specs=[
            pl.BlockSpec((1, tq, NSA_HPG * NSA_DH), lambda b, g, i: (b, i, g)),
            pl.BlockSpec((1, T, 2 * NSA_DH), lambda b, g, i: (b, 0, kvcol + g)),
            pl.BlockSpec((1, nsub + 1, R, LANES), lambda b, g, i: (g, 0, 0, 0)),
        ],
        out_specs=pl.BlockSpec((1, tq, NSA_HPG * NSA_DH), lambda b, g, i: (b, i, g)),
        scratch_shapes=[pltpu.VMEM((R, 2 * NSA_DH), bf16)],
        compiler_params=_cparams(("parallel", "parallel", "parallel")),
        name="nsa_win",
    )(z3, z3, tbl)


def _bucket_np(n):
    exact = N_BUCKETS // 2
    nf = np.maximum(n, 1).astype(np.float32)
    large = exact + (np.log(nf / exact) / math.log(MAX_DISTANCE / exact) * (N_BUCKETS - exact)).astype(np.int32)
    return np.where(n < exact, n, np.minimum(large, N_BUCKETS - 1)).astype(np.int32)


def _bucket_starts():
    b = _bucket_np(np.arange(4 * MAX_DISTANCE))
    return [int(np.argmax(b >= k)) for k in range(N_BUCKETS)]


def _t5_bias(rel_bias, dist, mask):
    n = jnp.maximum(dist, 0)[None]
    tbl = rel_bias.astype(f32).T
    col = lambda k: tbl[:, k].reshape((NSA_HEADS,) + (1,) * dist.ndim)
    out = jnp.broadcast_to(col(0), (NSA_HEADS,) + dist.shape)
    for k, start in enumerate(_bucket_starts()):
        if k:
            out = jnp.where(n >= start, col(k), out)
    return jnp.where(mask[None], out, NEG)


def _cmp_block_of_lane(W, n_cmp):
    lane = np.arange(2 * W)
    half = n_cmp // 2
    blk = np.where(lane < W, 2 * lane, 2 * (lane - W) + 1)
    ok = np.where(lane < W, lane < half, (lane - W) < half)
    return np.where(ok, blk, -1)


def _rope_tables(pos, half):
    inv = ROPE_BASE ** (-jnp.arange(half, dtype=f32) / half)
    ang = pos.astype(f32)[:, None] * inv[None, :]
    return jnp.cos(ang), jnp.sin(ang)


def _block_diag(blocks):
    n, r, c = blocks.shape[-3:]
    eye = jnp.eye(n, dtype=blocks.dtype)
    out = blocks[..., :, :, None, :] * eye[:, None, :, None]
    return out.reshape(blocks.shape[:-3] + (n * r, n * c))


def _pages_t(cache):
    L, pool, rows = cache.shape[:3]
    nd = cache.ndim
    return cache.transpose((0, 1) + tuple(range(3, nd)) + (2,)).reshape(L, pool, -1, rows)


def kernel(x_prompt, x_sample, cache_mla_ckv, cache_mla_krope, cache_nsa_cmp, cache_nsa_sel, page_table, state_nsa_win, state_ret, state_ffn_conv, rel_bias, norm_mix, norm_ffn, final_norm, ev_w_in, mla_q_norm, mla_w_qb, mla_kv_norm, mla_w_uk, mla_w_uv, ev_w_out, od_w_in, cmp_pe, cmp_w1, cmp_w2, od_w_out, ffn_w_up, ffn_conv_w, ffn_conv_b, ffn_w_down):
    B, T, D = x_prompt.shape
    DB, TS, _ = x_sample.shape
    depth = norm_mix.shape[0]
    npg = page_table.shape[1]
    past = npg * PAGE
    Mp, Ms = B * T, DB * TS
    assert T % LANES == 0 and NSA_G * TS == 8 and past % (2 * CMP_BLOCK) == 0 and TS < CMP_BLOCK
    assert past % SEL_BLOCK + TS <= SEL_BLOCK

    pos_p = jnp.arange(T)
    pos_s = past + jnp.arange(TS)
    pos_all = jnp.concatenate([jnp.tile(pos_p, B), jnp.tile(pos_s, DB)])
    cos16, sin16 = _rope_tables(pos_all, MLA_ROPE // 2)
    cos64, sin64 = _rope_tables(pos_all, RET_D // 2)
    c32 = jnp.tile(jnp.concatenate([cos16, cos16], axis=1), (1, LANES // MLA_ROPE))
    s32 = jnp.tile(jnp.concatenate([-sin16, sin16], axis=1), (1, LANES // MLA_ROPE))
    c64 = jnp.concatenate([cos64, cos64], axis=1)
    s64 = jnp.concatenate([-sin64, sin64], axis=1)

    h = jnp.concatenate([x_prompt.reshape(Mp, D), x_sample.reshape(Ms, D)], axis=0)

    tq = LANES
    n_blk_p = T // SEL_BLOCK
    Wp = 128
    assert n_blk_p <= Wp
    lane_blk = _cmp_block_of_lane(Wp, T // CMP_BLOCK)
    end = jnp.asarray(lane_blk * CMP_BLOCK + (CMP_BLOCK - 1))
    dist = jnp.arange(T)[:, None] - end[None, :]
    bias_cmp_p = _t5_bias(rel_bias, dist, (dist >= 0) & jnp.asarray(lane_blk >= 0)[None, :])

    def flash_table(nd, window):
        r = jnp.arange(tq)[None, :, None]
        c = jnp.arange(LANES)[None, None, :]
        dd = jnp.arange(nd + 1)[:, None, None] * LANES + r - c
        mask = (dd >= 0) & (jnp.arange(nd + 1)[:, None, None] < nd)
        if window:
            mask &= dd < WINDOW
        t = _t5_bias(rel_bias, dd, mask)
        t = t.reshape(NSA_G, NSA_HPG, nd + 1, tq, LANES).transpose(0, 2, 1, 3, 4)
        return t.reshape(NSA_G, nd + 1, NSA_HPG * tq, LANES)

    tbl_sel = flash_table(3, False)
    tbl_win = flash_table(WINDOW // LANES + 1, True)
    expand_pt = (jnp.arange(T)[:, None] // SEL_BLOCK == jnp.arange(Wp)[None, :]).astype(bf16)

    n_blk_s = -(-(past + TS) // SEL_BLOCK)
    Ws = max(128, -(-n_blk_s // 128) * 128)
    n_cmp_s = past // CMP_BLOCK
    assert n_cmp_s // 2 <= Ws
    lane_blk_s = _cmp_block_of_lane(Ws, n_cmp_s)
    end_s = jnp.asarray(lane_blk_s * CMP_BLOCK + (CMP_BLOCK - 1))
    qpos_s = past + jnp.arange(TS)
    dist_s = qpos_s[:, None] - end_s[None, :]
    bias_cmp_s = _t5_bias(rel_bias, dist_s, (dist_s >= 0) & jnp.asarray(lane_blk_s >= 0)[None, :])
    bias_cmp_s = bias_cmp_s.reshape(NSA_G, NSA_HPG, TS, 2 * Ws).transpose(0, 2, 1, 3).reshape(NSA_G, TS * NSA_HPG, 2 * Ws)
    expand_s = (jnp.arange(Ws)[:, None] == jnp.arange(past)[None, :] // SEL_BLOCK).astype(bf16)

    def sample_rows_table(kpos, window):
        dd = qpos_s[:, None] - kpos[None, :]
        mask = dd >= 0
        if window:
            mask &= (dd < WINDOW) & (kpos >= 0)[None, :]
        t = _t5_bias(rel_bias, dd, mask)
        t = t.reshape(NSA_G, NSA_HPG, TS, -1).transpose(1, 0, 2, 3)
        return t.reshape(NSA_HPG * NSA_G * TS, -1)

    bias_sel_s = sample_rows_table(jnp.arange(past), False)
    wb = state_nsa_win.shape[2]
    bias_win_s = sample_rows_table(past - wb + jnp.arange(wb), True)
    new_pos = jnp.concatenate([qpos_s, jnp.full((8 - TS,), 10 ** 9, qpos_s.dtype)])
    xb_nsa = sample_rows_table(new_pos, False)
    win_pages = jnp.arange(DB, dtype=jnp.int32).reshape(DB, 1)

    tt = np.repeat(np.arange(TS), MLA_HEADS)
    xb_mla = jnp.asarray(np.where(np.arange(8)[None, :] <= tt[:, None], 0.0, NEG).astype(np.float32))[None]

    krope_t = _pages_t(cache_mla_krope)
    cmp_t = _pages_t(cache_nsa_cmp)
    sel_t = _pages_t(cache_nsa_sel)
    win_t = _pages_t(state_nsa_win)

    outs = {k: [] for k in ("ckv_p", "ckv_s", "kr_p", "kr_s", "cmp_p", "cmp_s", "sel_p", "sel_s",
                            "win_p", "win_s", "ret_p", "ret_s", "conv_p", "conv_s")}

    def split(a):
        return a[:Mp], a[Mp:]

    def pad_rows(a, n):
        return jnp.pad(a, ((0, 0), (0, n - a.shape[1]), (0, 0)))

    for layer in range(depth):
        j = layer // 2
        if layer % 2 == 0:
            w = ev_w_in[j]
            a0 = MLA_Q_RANK + MLA_KV_RANK
            c0 = a0 + MLA_ROPE
            pad = jnp.zeros((D, EV_RET - c0), f32)
            w_p = jnp.concatenate([w[:, :c0], pad, w[:, c0:]], axis=1).astype(bf16)
            z = _mm(h, w_p, g=norm_mix[layer])
            wq = mla_w_qb[j].reshape(MLA_Q_RANK, MLA_HEADS, MLA_NOPE + MLA_ROPE)
            wq = jnp.concatenate([wq[:, :, :MLA_NOPE].reshape(MLA_Q_RANK, -1), wq[:, :, MLA_NOPE:].reshape(MLA_Q_RANK, -1)],
                                 axis=1).astype(bf16)
            q = _mm(z, wq, g=mla_q_norm[j])
            ckv, kr, qp, rq, rk = _even_post(z, q, mla_kv_norm[j], c32, s32, c64, s64)
            nr = RET_HEADS * RET_D
            nz = z.shape[1]
            z_p, z_s = split(z)

            wuk_t = mla_w_uk[j].transpose(1, 2, 0).astype(bf16)
            wuv = mla_w_uv[j].transpose(1, 0, 2).astype(bf16)
            q_p, q_s = split(q)
            qp_p, qp_s = split(qp)
            ckv_p, ckv_s = split(ckv)
            kr_p, kr_s = split(kr)
            kr_s = kr_s[:, :MLA_ROPE]

            o_mla_p = _mla_prompt(q_p.reshape(B, T, -1), qp_p.reshape(B, T, -1), ckv_p.reshape(B, T, -1),
                                  kr_p.reshape(B, T, -1), wuk_t, wuv).reshape(Mp, -1)
            wuk_bd = _block_diag(wuk_t)
            qa_s = (_mm(q_s, wuk_bd) * MLA_SCALE).reshape(DB, TS * MLA_HEADS, MLA_KV_RANK)
            o_lat = _paged_attn(page_table, cache_mla_ckv, j, qa_s,
                                pad_rows(ckv_s.reshape(DB, TS, -1), 8), xb_mla, False,
                                q2=qp_s.reshape(DB, TS * MLA_HEADS, MLA_ROPE), cache2=krope_t,
                                xk2=pad_rows(kr_s.reshape(DB, TS, -1), 8))
            wuv_bd = _block_diag(wuv)
            o_mla_s = _mm(o_lat.reshape(Ms, MLA_HEADS * MLA_KV_RANK), wuv_bd)

            rq_p, rq_s = split(rq)
            rk_p, rk_s = split(rk)
            z3 = z_p.reshape(B, T, nz)
            vcol = (EV_RET + 2 * nr) // nr
            ro_p, rs_p = _retention(rq_p.reshape(B, T, nr), rk_p.reshape(B, T, nr), z3,
                                    jnp.zeros((B, RET_HEADS, RET_D, RET_D), f32), float(min(RET_D, T)),
                                    vcol=vcol, gate=z3, gcol=vcol + 1)
            rv_s = z_s[:, EV_RET + 2 * nr:EV_RET + 3 * nr]
            rg_s = z_s[:, EV_RET + 3 * nr:]
            ro_s, rs_s = _retention(pad_rows(rq_s.reshape(DB, TS, nr), RET_D), pad_rows(rk_s.reshape(DB, TS, nr), RET_D),
                                    pad_rows(rv_s.reshape(DB, TS, nr), RET_D), state_ret[j], float(TS))
            ro_s = ro_s[:, :TS].reshape(Ms, RET_HEADS, RET_D)
            rn_s = ro_s * lax.rsqrt(jnp.mean(ro_s * ro_s, axis=-1, keepdims=True) + EPS)
            o_ret_s = jax.nn.silu(rg_s) * rn_s.reshape(Ms, nr)
            mix = jnp.concatenate([jnp.concatenate([o_mla_p, o_mla_s], axis=0),
                                   jnp.concatenate([ro_p.reshape(Mp, nr), o_ret_s], axis=0)], axis=1)
            h = _mm(mix, ev_w_out[j].astype(bf16), resid=h)

            outs["ckv_p"].append(ckv_p.reshape(B, T, -1))
            outs["ckv_s"].append(ckv_s.reshape(DB, TS, -1))
            outs["kr_p"].append(kr_p[:, :MLA_ROPE].reshape(B, T, -1))
            outs["kr_s"].append(kr_s.reshape(DB, TS, -1))
            outs["ret_p"].append(rs_p)
            outs["ret_s"].append(rs_s)
        else:
            w = od_w_in[j]
            nq_cols = NSA_HEADS * NSA_DH
            wkv = w[:, nq_cols:nq_cols + 3 * KV_LANES].reshape(D, 3, 2, NSA_G, NSA_DH)
            wsel = wkv[:, 1].transpose(0, 2, 1, 3).reshape(D, KV_LANES)
            wwin = wkv[:, 2].transpose(0, 2, 1, 3).reshape(D, KV_LANES)
            ngate = 3 * NSA_HEADS
            w_p = jnp.concatenate([w[:, :nq_cols + KV_LANES], wsel, wwin, w[:, nq_cols + 3 * KV_LANES:],
                                   jnp.zeros((D, 128 - ngate), f32)], axis=1).astype(bf16)
            z = _mm(h, w_p, g=norm_mix[layer])
            NZ = z.shape[1]
            c_cmp = nq_cols
            c_sel = c_cmp + KV_LANES
            c_win = c_sel + KV_LANES
            c_gate = c_win + KV_LANES
            z_p, z_s = split(z)
            z3 = z_p.reshape(B, T, NZ)
            gates = jax.nn.sigmoid(z[:, c_gate:c_gate + ngate]).reshape(-1, NSA_HEADS, 3)

            def orig_kv(a):
                return a.reshape(a.shape[0], NSA_G, 2, NSA_DH).transpose(0, 2, 1, 3)

            cmp_rows = z[:, c_cmp:c_sel].reshape(-1, 2, NSA_G, NSA_DH)
            sel_rows = orig_kv(z[:, c_sel:c_win])
            win_rows = orig_kv(z[:, c_win:c_gate])

            w1k = cmp_w1[j, 0].reshape(CMP_BLOCK, NSA_DH, -1)
            w1v = cmp_w1[j, 1].reshape(CMP_BLOCK, NSA_DH, -1)
            w1bd = _block_diag(jnp.stack([w1k, w1k, w1v, w1v], axis=1).astype(bf16))
            w2k, w2v = cmp_w2[j, 0], cmp_w2[j, 1]
            zb = jnp.zeros_like(w2k)
            w2p = jnp.concatenate([
                jnp.concatenate([w2k, zb, zb, zb], axis=1),
                jnp.concatenate([zb, zb, w2k, zb], axis=1),
                jnp.concatenate([zb, w2v, zb, zb], axis=1),
                jnp.concatenate([zb, zb, zb, w2v], axis=1)], axis=0).astype(bf16)
            pe_row = jnp.concatenate([cmp_pe[j, 0], cmp_pe[j, 0], cmp_pe[j, 1], cmp_pe[j, 1]], axis=1)

            kcv_p = _compress(z3, c_cmp // KV_LANES, pe_row, w1bd, w2p, Wp)
            oc_p, sel_p = _nsa_cmp_prompt(z3, kcv_p, bias_cmp_p, Wp, n_blk_p)
            os_p = _nsa_sel(z3, c_sel // LANES, tbl_sel, sel_p, expand_pt)
            ow_p = _nsa_win(z3, c_win // LANES, tbl_win)

            kcv_s = _compress_paged(page_table, cmp_t, j, pe_row, w1bd, w2p, Ws)
            q_s5 = z_s[:, :nq_cols].reshape(DB, TS, NSA_G, NSA_HPG, NSA_DH)
            q_tj = q_s5.transpose(0, 2, 1, 3, 4).reshape(DB, NSA_G, TS * NSA_HPG, NSA_DH)
            oc_s, sel_s, mexp_s = _nsa_cmp_sample(q_tj, kcv_s, bias_cmp_s, expand_s, Ws, n_blk_s, past)
            oc_s = oc_s.reshape(DB, NSA_G, TS, NSA_HPG, NSA_DH).transpose(0, 2, 1, 3, 4)
            q_jgt = q_s5.transpose(0, 3, 2, 1, 4) * NSA_SCALE
            lane_g = (jnp.arange(KV_LANES) // NSA_DH)[None, None, None, None, :] == jnp.arange(NSA_G)[None, None, :, None, None]
            q_aug = jnp.where(lane_g, jnp.tile(q_jgt, (1, 1, 1, 1, KV_LANES // NSA_DH)), 0.0)
            q_aug = q_aug.reshape(DB, NSA_HPG * NSA_G * TS, KV_LANES)
            new_sel = sel_rows[Mp:].reshape(DB, TS, KV_LANES)
            new_win = win_rows[Mp:].reshape(DB, TS, KV_LANES)
            cur_blk = past // SEL_BLOCK
            member = sel_s[:, :, cur_blk].reshape(DB, 1, NSA_G * TS)
            member = jnp.broadcast_to(member, (DB, NSA_HPG, NSA_G * TS)).reshape(DB, -1, 1)
            xb_sel = jnp.where(member > 0.5, xb_nsa[None], NEG)
            o_sel = _paged_attn(page_table, sel_t, j, q_aug, pad_rows(new_sel, 8), xb_sel, True,
                                bias=bias_sel_s, mask=mexp_s)
            o_win = _paged_attn(win_pages, win_t, j, q_aug, pad_rows(new_win, 8), xb_nsa[None], True,
                                bias=bias_win_s)

            def unpack(o):
                o = o.reshape(DB, NSA_HPG, NSA_G, TS, 2, NSA_G, NSA_DH)[:, :, :, :, 1]
                o = jnp.stack([o[:, :, g, :, g] for g in range(NSA_G)], axis=2)
                return o.transpose(0, 3, 2, 1, 4)

            os_s, ow_s = unpack(o_sel), unpack(o_win)

            def cat(p_, s_):
                return jnp.concatenate([p_.reshape(Mp, NSA_HEADS, NSA_DH), s_.reshape(Ms, NSA_HEADS, NSA_DH)], axis=0)

            o = (gates[..., 0:1] * cat(oc_p, oc_s) + gates[..., 1:2] * cat(os_p, os_s)
                 + gates[..., 2:3] * cat(ow_p, ow_s)).reshape(-1, nq_cols)
            h = _mm(o, od_w_out[j].astype(bf16), resid=h)

            wrows = min(WINDOW, T)
            outs["cmp_p"].append(cmp_rows[:Mp].reshape(B, T, 2, NSA_G, NSA_DH))
            outs["cmp_s"].append(cmp_rows[Mp:].reshape(DB, TS, 2, NSA_G, NSA_DH))
            outs["sel_p"].append(sel_rows[:Mp].reshape(B, T, 2, NSA_G, NSA_DH))
            outs["sel_s"].append(sel_rows[Mp:].reshape(DB, TS, 2, NSA_G, NSA_DH))
            outs["win_p"].append(win_rows[:Mp].reshape(B, T, 2, NSA_G, NSA_DH)[:, T - wrows:])
            outs["win_s"].append(win_rows[Mp:].reshape(DB, TS, 2, NSA_G, NSA_DH))

        up = _mm(h, ffn_w_up[layer].astype(bf16), g=norm_ffn[layer])
        w_down = ffn_w_down[layer].astype(bf16)
        a_s = up[Mp:, :D_FF].reshape(DB, TS, D_FF)
        a_ext_s = jnp.concatenate([state_ffn_conv[layer], a_s], axis=1)
        h_p = _ffn_down_prompt(up, ffn_conv_w[layer], ffn_conv_b[layer], w_down, h, Mp, T)
        h_s = _ffn_down_rows(up, a_ext_s[:, 1:1 + TS].reshape(Ms, D_FF), a_ext_s[:, 0:TS].reshape(Ms, D_FF),
                             ffn_conv_w[layer], ffn_conv_b[layer], w_down, h, Mp)
        h = jnp.concatenate([h_p, h_s], axis=0)
        last2 = (np.arange(B)[:, None] * T + np.arange(T - 2, T)[None, :]).reshape(-1)
        outs["conv_p"].append(up[last2][:, :D_FF].reshape(B, 2, D_FF))
        outs["conv_s"].append(a_ext_s[:, TS:])

    y = _rmsnorm(h, final_norm)
    st = lambda k: jnp.stack(outs[k])
    return (y[:Mp].reshape(B, T, D), y[Mp:].reshape(DB, TS, D),
            st("ckv_p"), st("ckv_s"), st("kr_p"), st("kr_s"),
            st("cmp_p"), st("cmp_s"), st("sel_p"), st("sel_s"),
            st("win_p"), jnp.concatenate([state_nsa_win[:, :, TS:], st("win_s")], axis=2), st("ret_p"), st("ret_s"),
            st("conv_p"), st("conv_s"))
```

```python
import functools
import math

import numpy as np
import jax
import jax.numpy as jnp
from jax import lax
from jax.experimental import pallas as pl
from jax.experimental.pallas import tpu as pltpu

f32 = jnp.float32
bf16 = jnp.bfloat16

PAGE = 128
MLA_HEADS = 8
MLA_Q_RANK = 384
MLA_KV_RANK = 256
MLA_NOPE = 64
MLA_ROPE = 32
MLA_V = 64
MLA_SCALE = (MLA_NOPE + MLA_ROPE) ** -0.5
RET_HEADS = 4
RET_D = 128
NSA_HEADS = 16
NSA_G = 2
NSA_HPG = 8
NSA_DH = 64
NSA_SCALE = NSA_DH ** -0.5
CMP_BLOCK = 32
SEL_BLOCK = 64
N_SEL = 16
WINDOW = 512
D_FF = 2816
N_BUCKETS = 32
MAX_DISTANCE = 128
ROPE_BASE = 10000.0
EPS = 1e-6
NEG = -1e30
FORCE = 1e4
KV_LANES = 2 * NSA_G * NSA_DH
LANES = 128
FLASH_TK = 1024
XS_PITCH = 72

VMEM_LIMIT = 56 * 1024 * 1024


def _cparams(sem):
    return pltpu.CompilerParams(dimension_semantics=sem, vmem_limit_bytes=VMEM_LIMIT)


def _dot(a, b):
    return jnp.dot(a, b, preferred_element_type=f32)


def _dot_nt(a, b):
    return lax.dot_general(a, b, (((1,), (1,)), ((), ())), preferred_element_type=f32)


def _row_tile(m, cap):
    for t in (512, 256, 128, 64, 32, 16, 8):
        if t <= cap and m % t == 0:
            return t
    return m


def _pick_tile(n, cap):
    best = None
    for t in range(128, min(n, cap) + 1, 128):
        if n % t == 0:
            best = t
    return best if best is not None else n


def _mm_kernel(*refs, norm, resid):
    it = iter(refs)
    x_ref = next(it)
    g_ref = next(it) if norm else None
    w_ref = next(it)
    r_ref = next(it) if resid else None
    o_ref = next(it)
    xn_ref = next(it)

    @pl.when(pl.program_id(1) == 0)
    def _():
        x = x_ref[...]
        if norm:
            x = x * lax.rsqrt(jnp.mean(x * x, axis=-1, keepdims=True) + EPS) * g_ref[...]
        xn_ref[...] = x.astype(bf16)

    acc = _dot(xn_ref[...], w_ref[...])
    if resid:
        acc = acc + r_ref[...]
    o_ref[...] = acc


def _mm(x, w, g=None, resid=None, tm=512):
    M = x.shape[0]
    K, N = w.shape
    tm = _row_tile(M, tm)
    assert K == x.shape[1] or K % 128 == 0
    tn = _pick_tile(N, 1536)
    norm = g is not None
    has_r = resid is not None
    in_specs = [pl.BlockSpec((tm, K), lambda i, j: (i, 0))]
    args = [x]
    if norm:
        in_specs.append(pl.BlockSpec((1, K), lambda i, j: (0, 0)))
        args.append(g.reshape(1, K).astype(f32))
    in_specs.append(pl.BlockSpec((K, tn), lambda i, j: (0, j)))
    args.append(w)
    if has_r:
        in_specs.append(pl.BlockSpec((tm, tn), lambda i, j: (i, j)))
        args.append(resid)
    return pl.pallas_call(
        functools.partial(_mm_kernel, norm=norm, resid=has_r),
        out_shape=jax.ShapeDtypeStruct((M, N), f32),
        grid=(M // tm, N // tn),
        in_specs=in_specs,
        out_specs=pl.BlockSpec((tm, tn), lambda i, j: (i, j)),
        scratch_shapes=[pltpu.VMEM((tm, K), bf16)],
        compiler_params=_cparams(("parallel", "arbitrary")),
        name="mm",
    )(*args)


def _rmsnorm_kernel(x_ref, g_ref, o_ref):
    x = x_ref[...]
    o_ref[...] = x * lax.rsqrt(jnp.mean(x * x, axis=-1, keepdims=True) + EPS) * g_ref[...]


def _rmsnorm(x, g, tm=512):
    M, D = x.shape
    tm = _row_tile(M, tm)
    return pl.pallas_call(
        _rmsnorm_kernel,
        out_shape=jax.ShapeDtypeStruct((M, D), f32),
        grid=(M // tm,),
        in_specs=[pl.BlockSpec((tm, D), lambda i: (i, 0)), pl.BlockSpec((1, D), lambda i: (0, 0))],
        out_specs=pl.BlockSpec((tm, D), lambda i: (i, 0)),
        compiler_params=_cparams(("parallel",)),
        name="rmsnorm",
    )(x, g.reshape(1, D))


EV_CKV = MLA_Q_RANK
EV_KR = MLA_Q_RANK + MLA_KV_RANK
EV_RET = 1024
EV_N = EV_RET + 4 * RET_HEADS * RET_D


def _even_post_kernel(z_ref, q_ref, g_ref, c32_ref, s32_ref, c64_ref, s64_ref,
                      ckv_ref, kr_ref, qp_ref, rq_ref, rk_ref):
    zc = z_ref[:, EV_CKV:EV_KR]
    ckv_ref[...] = zc * lax.rsqrt(jnp.mean(zc * zc, axis=-1, keepdims=True) + EPS) * g_ref[...]

    c32, s32 = c32_ref[...], s32_ref[...]
    lane = lax.broadcasted_iota(jnp.int32, c32.shape, 1)
    first_half = lax.rem(lane, MLA_ROPE) < MLA_ROPE // 2

    def rope32(x):
        swapped = jnp.where(first_half, pltpu.roll(x, LANES - MLA_ROPE // 2, 1), pltpu.roll(x, MLA_ROPE // 2, 1))
        return x * c32 + swapped * s32

    kr_ref[...] = rope32(z_ref[:, EV_KR:EV_KR + LANES])
    q0 = MLA_HEADS * MLA_NOPE
    for t in range(MLA_HEADS * MLA_ROPE // LANES):
        qp_ref[:, t * LANES:(t + 1) * LANES] = rope32(q_ref[:, q0 + t * LANES:q0 + (t + 1) * LANES]) * MLA_SCALE

    c64, s64 = c64_ref[...], s64_ref[...]
    nr = RET_HEADS * RET_D
    for h in range(RET_HEADS):
        xq = z_ref[:, EV_RET + h * RET_D:EV_RET + (h + 1) * RET_D]
        rq_ref[:, h * RET_D:(h + 1) * RET_D] = xq * c64 + pltpu.roll(xq, RET_D // 2, 1) * s64
        xk = z_ref[:, EV_RET + nr + h * RET_D:EV_RET + nr + (h + 1) * RET_D]
        rk_ref[:, h * RET_D:(h + 1) * RET_D] = (xk * c64 + pltpu.roll(xk, RET_D // 2, 1) * s64) * RET_D ** -0.5


def _even_post(z, q, kv_gain, c32, s32, c64, s64, row0, nrows, tm=256):
    tm = _row_tile(math.gcd(nrows, row0) if row0 else nrows, tm)
    off = row0 // tm
    nr = RET_HEADS * RET_D
    src = lambda i: (i + off, 0)
    dst = lambda i: (i, 0)
    widths = (MLA_KV_RANK, LANES, MLA_HEADS * MLA_ROPE, nr, nr)
    return pl.pallas_call(
        _even_post_kernel,
        out_shape=tuple(jax.ShapeDtypeStruct((nrows, w), f32) for w in widths),
        grid=(nrows // tm,),
        in_specs=[pl.BlockSpec((tm, EV_N), src), pl.BlockSpec((tm, q.shape[1]), src),
                  pl.BlockSpec((1, MLA_KV_RANK), lambda i: (0, 0))]
        + [pl.BlockSpec((tm, LANES), src)] * 4,
        out_specs=tuple(pl.BlockSpec((tm, w), dst) for w in widths),
        compiler_params=_cparams(("parallel",)),
        name="even_post",
    )(z, q, kv_gain.reshape(1, MLA_KV_RANK), c32, s32, c64, s64)


def _ffn_gate_down(a, a1, a2, u, cw_ref, cb_ref, w_ref, r_ref, o_ref):
    cw = cw_ref[...]
    c = cb_ref[...] + a2 * cw[0:1] + a1 * cw[1:2] + a * cw[2:3]
    gate = c * jax.nn.sigmoid(c) * u
    o_ref[...] = _dot(gate.astype(bf16), w_ref[...]) + r_ref[...]


def _ffn_down_prompt_kernel(a_ref, prev_ref, u_ref, cw_ref, cb_ref, w_ref, r_ref, o_ref, *, tiles_per_seq):
    a = a_ref[...]
    keep = jnp.where(lax.rem(pl.program_id(0), tiles_per_seq) == 0, 0.0, 1.0)
    p7 = prev_ref[7:8, :] * keep
    p6 = prev_ref[6:7, :] * keep
    row = lax.broadcasted_iota(jnp.int32, a.shape, 0)
    a1 = jnp.where(row == 0, p7, pltpu.roll(a, 1, 0))
    a2 = jnp.where(row == 0, p6, jnp.where(row == 1, p7, pltpu.roll(a, 2, 0)))
    _ffn_gate_down(a, a1, a2, u_ref[...], cw_ref, cb_ref, w_ref, r_ref, o_ref)


def _ffn_down_prompt(up, conv_w, conv_b, w_down, resid, Mp, T, tm=256):
    D = w_down.shape[1]
    tm = _row_tile(T, tm)
    row = lambda i: (i, 0)
    return pl.pallas_call(
        functools.partial(_ffn_down_prompt_kernel, tiles_per_seq=T // tm),
        out_shape=jax.ShapeDtypeStruct((Mp, D), f32),
        grid=(Mp // tm,),
        in_specs=[
            pl.BlockSpec((tm, D_FF), row),
            pl.BlockSpec((8, D_FF), lambda i: (jnp.maximum(i * (tm // 8) - 1, 0), 0)),
            pl.BlockSpec((tm, D_FF), lambda i: (i, 1)),
            pl.BlockSpec((3, D_FF), lambda i: (0, 0)),
            pl.BlockSpec((1, D_FF), lambda i: (0, 0)),
            pl.BlockSpec((D_FF, D), lambda i: (0, 0)),
            pl.BlockSpec((tm, D), row),
        ],
        out_specs=pl.BlockSpec((tm, D), row),
        compiler_params=_cparams(("parallel",)),
        name="ffn_down_prompt",
    )(up, up, up, conv_w, conv_b.reshape(1, D_FF), w_down, resid)


def _ffn_down_rows_kernel(a_ref, a1_ref, a2_ref, u_ref, cw_ref, cb_ref, w_ref, r_ref, o_ref):
    _ffn_gate_down(a_ref[...], a1_ref[...], a2_ref[...], u_ref[...], cw_ref, cb_ref, w_ref, r_ref, o_ref)


def _ffn_down_rows(up, a1, a2, conv_w, conv_b, w_down, resid, row0, tm=256):
    Ms = a1.shape[0]
    D = w_down.shape[1]
    tm = _row_tile(math.gcd(Ms, row0) if row0 else Ms, tm)
    off = row0 // tm
    row = lambda i: (i, 0)
    return pl.pallas_call(
        _ffn_down_rows_kernel,
        out_shape=jax.ShapeDtypeStruct((Ms, D), f32),
        grid=(Ms // tm,),
        in_specs=[
            pl.BlockSpec((tm, D_FF), lambda i: (i + off, 0)),
            pl.BlockSpec((tm, D_FF), row),
            pl.BlockSpec((tm, D_FF), row),
            pl.BlockSpec((tm, D_FF), lambda i: (i + off, 1)),
            pl.BlockSpec((3, D_FF), lambda i: (0, 0)),
            pl.BlockSpec((1, D_FF), lambda i: (0, 0)),
            pl.BlockSpec((D_FF, D), lambda i: (0, 0)),
            pl.BlockSpec((tm, D), lambda i: (i + off, 0)),
        ],
        out_specs=pl.BlockSpec((tm, D), row),
        compiler_params=_cparams(("parallel",)),
        name="ffn_down_rows",
    )(up, a1, a2, up, conv_w, conv_b.reshape(1, D_FF), w_down, resid)


def _retention_kernel(*refs, gated):
    if gated:
        q_ref, k_ref, v_ref, g_ref, dm_ref, qd_ref, kd_ref, cd_ref, s0_ref, o_ref, so_ref, s_scr = refs
    else:
        q_ref, k_ref, v_ref, dm_ref, qd_ref, kd_ref, cd_ref, s0_ref, o_ref, so_ref, s_scr = refs
    c = pl.program_id(1)

    @pl.when(c == 0)
    def _():
        s_scr[...] = s0_ref[0]

    for h in range(RET_HEADS):
        lanes = slice(h * RET_D, (h + 1) * RET_D)
        q = q_ref[0, :, lanes]
        k = k_ref[0, :, lanes]
        v = v_ref[0, :, lanes].astype(bf16)
        S = s_scr[h]
        inner = _dot_nt(q.astype(bf16), k.astype(bf16)) * dm_ref[h]
        o = _dot(inner.astype(bf16), v) + _dot((q * qd_ref[h]).astype(bf16), S.astype(bf16))
        if gated:
            gate = g_ref[0, :, lanes]
            o = o * lax.rsqrt(jnp.mean(o * o, axis=-1, keepdims=True) + EPS) * (gate * jax.nn.sigmoid(gate))
        o_ref[0, :, lanes] = o
        kd = (k * kd_ref[h]).T
        s_new = S * cd_ref[h, 0:1, :] + _dot(kd.astype(bf16), v)
        s_scr[h] = s_new
        so_ref[0, h] = s_new


def _retention(q, k, v, s0, length, vcol=0, gate=None, gcol=0, chunk_major=False):
    Bn, Tn, _ = q.shape
    C = RET_D
    nc = Tn // C
    H = RET_HEADS
    log_g = jnp.log(1.0 - 2.0 ** (-5.0 - jnp.arange(H, dtype=f32)))
    idx = jnp.arange(C, dtype=f32)
    diff = idx[:, None] - idx[None, :]
    dmask = jnp.where(diff >= 0, jnp.exp(jnp.maximum(diff, 0.0) * log_g[:, None, None]), 0.0)
    q_dec = jnp.exp((idx + 1.0) * log_g[:, None])
    k_dec = jnp.where(idx < length, jnp.exp(jnp.maximum(length - 1.0 - idx, 0.0) * log_g[:, None]), 0.0)
    c_dec = jnp.exp(length * log_g)
    qd = jnp.broadcast_to(q_dec[:, :, None], (H, C, C))
    kd = jnp.broadcast_to(k_dec[:, :, None], (H, C, C))
    cd = jnp.broadcast_to(c_dec[:, None, None], (H, 8, C))
    qkv_spec = pl.BlockSpec((1, C, H * C), lambda b, c: (b, c, 0))
    tbl_spec = pl.BlockSpec((H, C, C), lambda b, c: (0, 0, 0))
    st_spec = pl.BlockSpec((1, H, C, C), lambda b, c: (b, 0, 0, 0))
    gated = gate is not None
    if chunk_major:
        vspec = lambda col: pl.BlockSpec((1, C, H * C), lambda b, c: (b * nc + c, 0, col))
    else:
        vspec = lambda col: pl.BlockSpec((1, C, H * C), lambda b, c: (b, c, col))
    in_specs = [qkv_spec, qkv_spec, vspec(vcol)]
    args = [q, k, v]
    if gated:
        in_specs.append(vspec(gcol))
        args.append(gate)
    in_specs += [tbl_spec, tbl_spec, tbl_spec, pl.BlockSpec((H, 8, C), lambda b, c: (0, 0, 0)), st_spec]
    return pl.pallas_call(
        functools.partial(_retention_kernel, gated=gated),
        out_shape=(jax.ShapeDtypeStruct((Bn, Tn, H * C), f32), jax.ShapeDtypeStruct((Bn, H, C, C), f32)),
        grid=(Bn, nc),
        in_specs=in_specs,
        out_specs=(qkv_spec, st_spec),
        scratch_shapes=[pltpu.VMEM((H, C, C), f32)],
        compiler_params=_cparams(("parallel", "arbitrary")),
        name="retention",
    )(*args, dmask, qd, kd, cd, s0)


def _pairs(nq, jlo, jhi):
    ii, jj, fl = [], [], []
    for i in range(nq):
        lo, hi = jlo(i), jhi(i)
        for j in range(lo, hi + 1):
            ii.append(i)
            jj.append(j)
            fl.append((1 if j == lo else 0) | (2 if j == hi else 0))
    return (jnp.asarray(np.array(ii, np.int32)), jnp.asarray(np.array(jj, np.int32)),
            jnp.asarray(np.array(fl, np.int32)))


def _add_bias_tiles(s, tbl_ref, i, tile0, nd):
    parts = []
    for c in range(s.shape[1] // LANES):
        d = i - (tile0 + c)
        idx = jnp.where(d < 0, nd, jnp.minimum(d, nd - 1))
        parts.append(s[:, c * LANES:(c + 1) * LANES] + tbl_ref[0, idx])
    return jnp.concatenate(parts, axis=1)


def _flash_init(m_s, l_s, acc_s):
    m_s[...] = jnp.full(m_s.shape, NEG, f32)
    l_s[...] = jnp.zeros(l_s.shape, f32)
    acc_s[...] = jnp.zeros(acc_s.shape, f32)


def _flash_update(s, v, m_s, l_s, acc_s):
    m_prev = m_s[...]
    m_new = jnp.maximum(m_prev, jnp.max(s, axis=-1, keepdims=True))
    alpha = jnp.exp(m_prev - m_new)
    pr = jnp.exp(s - m_new)
    l_s[...] = alpha * l_s[...] + jnp.sum(pr, axis=-1, keepdims=True)
    acc_s[...] = alpha * acc_s[...] + _dot(pr.astype(bf16), v)
    m_s[...] = m_new


def _mla_prompt_kernel(ii_ref, jj_ref, fl_ref, qn_ref, qp_ref, wuk_ref, ckv_ref, kr_ref, tbl_ref, wuv_ref, o_ref,
                       qa_s, qp_s, m_s, l_s, acc_s, *, tq, tk):
    p = pl.program_id(1)
    i = ii_ref[p]
    j = jj_ref[p]
    fl = fl_ref[p]
    H = MLA_HEADS

    @pl.when((fl & 1) == 1)
    def _():
        qp_s[...] = jnp.zeros(qp_s.shape, bf16)
        for h in range(H):
            qh = qn_ref[0, :, h * MLA_NOPE:(h + 1) * MLA_NOPE].astype(bf16)
            qa = _dot(qh, wuk_ref[h]) * MLA_SCALE
            qa_s[h * tq:(h + 1) * tq, :] = qa.astype(bf16)
            qp_s[h * tq:(h + 1) * tq, 0:MLA_ROPE] = qp_ref[0, :, h * MLA_ROPE:(h + 1) * MLA_ROPE].astype(bf16)
        _flash_init(m_s, l_s, acc_s)

    kv = ckv_ref[0].astype(bf16)
    kr = kr_ref[0].astype(bf16)
    s = _dot_nt(qa_s[...], kv) + _dot_nt(qp_s[...], kr)
    s = _add_bias_tiles(s, tbl_ref, i, j * (tk // LANES), tbl_ref.shape[1] - 1)
    _flash_update(s, kv, m_s, l_s, acc_s)

    @pl.when((fl & 2) == 2)
    def _():
        o = acc_s[...] / l_s[...]
        for h in range(H):
            o_ref[0, :, h * MLA_V:(h + 1) * MLA_V] = _dot(o[h * tq:(h + 1) * tq].astype(bf16), wuv_ref[h])


def _mla_prompt(qn, qp, ckv, kr, wuk_t, wuv, tk=FLASH_TK):
    B, T, _ = qp.shape
    tq = LANES
    tk = min(tk, T)
    H = MLA_HEADS
    R = H * tq
    nq = T // tq
    ii, jj, fl = _pairs(T // tq, lambda i: 0, lambda i: (i * tq + tq - 1) // tk)
    npairs = int(ii.shape[0])
    r = np.arange(tq)[:, None]
    c = np.arange(LANES)[None, :]
    causal = np.where(c <= r, 0.0, NEG).astype(np.float32)
    tbl = np.stack([np.tile(causal, (H, 1)), np.zeros((R, LANES), np.float32), np.full((R, LANES), NEG, np.float32)])
    grid_spec = pltpu.PrefetchScalarGridSpec(
        num_scalar_prefetch=3,
        grid=(B, npairs),
        in_specs=[
            pl.BlockSpec((1, tq, H * MLA_NOPE), lambda b, p, ii, jj, fl: (b * nq + ii[p], 0, 0)),
            pl.BlockSpec((1, tq, H * MLA_ROPE), lambda b, p, ii, jj, fl: (b, ii[p], 0)),
            pl.BlockSpec((H, MLA_NOPE, MLA_KV_RANK), lambda b, p, ii, jj, fl: (0, 0, 0)),
            pl.BlockSpec((1, tk, MLA_KV_RANK), lambda b, p, ii, jj, fl: (b, jj[p], 0)),
            pl.BlockSpec((1, tk, LANES), lambda b, p, ii, jj, fl: (b, jj[p], 0)),
            pl.BlockSpec((1, 3, R, LANES), lambda b, p, ii, jj, fl: (0, 0, 0, 0)),
            pl.BlockSpec((H, MLA_KV_RANK, MLA_V), lambda b, p, ii, jj, fl: (0, 0, 0)),
        ],
        out_specs=pl.BlockSpec((1, tq, H * MLA_V), lambda b, p, ii, jj, fl: (b, ii[p], 0)),
        scratch_shapes=[
            pltpu.VMEM((R, MLA_KV_RANK), bf16),
            pltpu.VMEM((R, LANES), bf16),
            pltpu.VMEM((R, 1), f32),
            pltpu.VMEM((R, 1), f32),
            pltpu.VMEM((R, MLA_KV_RANK), f32),
        ],
    )
    return pl.pallas_call(
        functools.partial(_mla_prompt_kernel, tq=tq, tk=tk),
        out_shape=jax.ShapeDtypeStruct((B, T, H * MLA_V), f32),
        grid_spec=grid_spec,
        compiler_params=_cparams(("parallel", "arbitrary")),
        name="mla_prompt",
    )(ii, jj, fl, qn, qp, wuk_t, ckv, kr, jnp.asarray(tbl)[None], wuv)


def _page_copies(pt_ref, cache, buf, sem, layer, npg, seq, slot, transposed):
    cps = []
    for p in range(npg):
        src = cache.at[layer, pt_ref[seq * npg + p]]
        if len(buf.shape) == 4:
            dst = buf.at[slot, p]
        elif transposed:
            pw = cache.shape[3]
            dst = buf.at[slot, :, pl.ds(p * pw, pw)]
        else:
            pw = cache.shape[2]
            dst = buf.at[slot, pl.ds(p * pw, pw)]
        cps.append(pltpu.make_async_copy(src, dst, sem.at[slot]))
    return cps


def _gather_step(pt_ref, caches, bufs, sems, transposed, layer, npg):
    b = pl.program_id(0)
    nb = pl.num_programs(0)

    def copies(seq, slot):
        out = []
        for cache, buf, sem, tr in zip(caches, bufs, sems, transposed):
            out += _page_copies(pt_ref, cache, buf, sem, layer, npg, seq, slot, tr)
        return out

    @pl.when(b == 0)
    def _():
        for cp in copies(0, 0):
            cp.start()

    @pl.when(b + 1 < nb)
    def _():
        for cp in copies(b + 1, lax.rem(b + 1, 2)):
            cp.start()

    slot = lax.rem(b, 2)
    for cp in copies(b, slot):
        cp.wait()
    return slot


def _paged_attn_kernel(pt_ref, *refs, layer, npg, main_t, has2, has_bias, has_mask):
    it = iter(refs)
    q_ref = next(it)
    q2_ref = next(it) if has2 else None
    bias_ref = next(it) if has_bias else None
    mask_ref = next(it) if has_mask else None
    xk_ref = next(it)
    xk2_ref = next(it) if has2 else None
    xb_ref = next(it)
    cache = next(it)
    cache2 = next(it) if has2 else None
    o_ref = next(it)
    buf = next(it)
    buf2 = next(it) if has2 else None
    sem = next(it)
    sem2 = next(it) if has2 else None

    caches, bufs, sems, trs = [cache], [buf], [sem], [main_t]
    if has2:
        caches.append(cache2)
        bufs.append(buf2)
        sems.append(sem2)
        trs.append(True)
    slot = _gather_step(pt_ref, caches, bufs, sems, trs, layer, npg)

    q = q_ref[0].astype(bf16)
    pages = buf[slot].astype(bf16)
    xk = xk_ref[0].astype(bf16)
    s = _dot(q, pages) if main_t else _dot_nt(q, pages)
    sx = _dot_nt(q, xk)
    if has2:
        q2 = q2_ref[0].astype(bf16)
        s = s + _dot(q2, buf2[slot].astype(bf16))
        sx = sx + _dot_nt(q2, xk2_ref[0].astype(bf16))
    if has_bias:
        s = s + bias_ref[...]
    if has_mask:
        reps = s.shape[0] // mask_ref.shape[1]
        s = s + jnp.concatenate([mask_ref[0]] * reps, axis=0)
    sx = sx + xb_ref[0]
    m = jnp.maximum(jnp.max(s, axis=-1, keepdims=True), jnp.max(sx, axis=-1, keepdims=True))
    pr = jnp.exp(s - m)
    px = jnp.exp(sx - m)
    l = jnp.sum(pr, axis=-1, keepdims=True) + jnp.sum(px, axis=-1, keepdims=True)
    pb = pr.astype(bf16)
    o = (_dot_nt(pb, pages) if main_t else _dot(pb, pages)) + _dot(px.astype(bf16), xk)
    o_ref[0] = jnp.where(m > 0.5 * NEG, o / l, 0.0)


def _paged_attn(page_table, cache, layer, q, xk, xb, main_t, q2=None, cache2=None, xk2=None, bias=None, mask=None):
    DB, R, DL = q.shape
    npg = page_table.shape[1]
    pw = cache.shape[3] if main_t else cache.shape[2]
    K = npg * pw
    has2 = q2 is not None
    seq = lambda b, pt: (b, 0, 0)
    in_specs = [pl.BlockSpec((1, R, DL), seq)]
    args = [q]
    if has2:
        d2 = q2.shape[-1]
        in_specs.append(pl.BlockSpec((1, R, d2), seq))
        args.append(q2)
    if bias is not None:
        in_specs.append(pl.BlockSpec((R, K), lambda b, pt: (0, 0)))
        args.append(bias)
    if mask is not None:
        in_specs.append(pl.BlockSpec((1, 8, K), seq))
        args.append(mask)
    in_specs.append(pl.BlockSpec((1, 8, DL), seq))
    args.append(xk)
    if has2:
        in_specs.append(pl.BlockSpec((1, 8, d2), seq))
        args.append(xk2)
    if xb.shape[0] == 1:
        in_specs.append(pl.BlockSpec((1, R, 8), lambda b, pt: (0, 0, 0)))
    else:
        in_specs.append(pl.BlockSpec((1, R, 8), seq))
    args.append(xb)
    in_specs.append(pl.BlockSpec(memory_space=pl.ANY))
    args.append(cache)
    scratch = [pltpu.VMEM((2, DL, K) if main_t else (2, K, DL), f32)]
    if has2:
        in_specs.append(pl.BlockSpec(memory_space=pl.ANY))
        args.append(cache2)
        scratch.append(pltpu.VMEM((2, d2, K), f32))
    scratch.append(pltpu.SemaphoreType.DMA((2,)))
    if has2:
        scratch.append(pltpu.SemaphoreType.DMA((2,)))
    grid_spec = pltpu.PrefetchScalarGridSpec(
        num_scalar_prefetch=1,
        grid=(DB,),
        in_specs=in_specs,
        out_specs=pl.BlockSpec((1, R, DL), seq),
        scratch_shapes=scratch,
    )
    return pl.pallas_call(
        functools.partial(_paged_attn_kernel, layer=layer, npg=npg, main_t=main_t, has2=has2,
                          has_bias=bias is not None, has_mask=mask is not None),
        out_shape=jax.ShapeDtypeStruct((DB, R, DL), f32),
        grid_spec=grid_spec,
        compiler_params=_cparams(("arbitrary",)),
        name="paged_attn",
    )(page_table.reshape(-1), *args)


def _compress_compute(load_rows, pe_ref, w1_ref, w2_ref, o_ref, nblk, W):
    half = nblk // 2
    acc = jnp.zeros((nblk, KV_LANES), f32)
    for r in range(CMP_BLOCK):
        x = jnp.concatenate([load_rows(r, half), load_rows(CMP_BLOCK + r, half)], axis=0) + pe_ref[r:r + 1, :]
        acc = acc + _dot(x.astype(bf16), w1_ref[r])
    hdn = acc * jax.nn.sigmoid(acc)
    out = _dot(hdn.astype(bf16), w2_ref[...])
    o_ref[0, 0:half, :] = out[:half]
    o_ref[0, W:W + half, :] = out[half:]
    if half < W:
        zeros = jnp.zeros((W - half, KV_LANES), f32)
        o_ref[0, half:W, :] = zeros
        o_ref[0, W + half:2 * W, :] = zeros


def _compress_kernel(xlo_ref, xhi_ref, pe_ref, w1_ref, w2_ref, o_ref, *, nblk, W):
    def load(r, n):
        rows = pl.ds(r, n, stride=2 * CMP_BLOCK)
        return jnp.concatenate([xlo_ref[0, rows, :], xhi_ref[0, rows, :]], axis=1)

    _compress_compute(load, pe_ref, w1_ref, w2_ref, o_ref, nblk, W)


def _compress_weight_specs(idx):
    return [
        pl.BlockSpec((CMP_BLOCK, KV_LANES), idx(2)),
        pl.BlockSpec((CMP_BLOCK, KV_LANES, KV_LANES), idx(3)),
        pl.BlockSpec((KV_LANES, KV_LANES), idx(2)),
    ]


def _compress(z3, colblk, pe_row, w1bd, w2p, W):
    B, T, _ = z3.shape
    nblk = T // CMP_BLOCK
    return pl.pallas_call(
        functools.partial(_compress_kernel, nblk=nblk, W=W),
        out_shape=jax.ShapeDtypeStruct((B, 2 * W, KV_LANES), f32),
        grid=(B,),
        in_specs=[pl.BlockSpec((1, T, LANES), lambda b: (b, 0, 2 * colblk)),
                  pl.BlockSpec((1, T, LANES), lambda b: (b, 0, 2 * colblk + 1))]
        + _compress_weight_specs(lambda n: (lambda b: (0,) * n)),
        out_specs=pl.BlockSpec((1, 2 * W, KV_LANES), lambda b: (b, 0, 0)),
        compiler_params=_cparams(("parallel",)),
        name="compress",
    )(z3, z3, pe_row, w1bd, w2p)


def _compress_paged_kernel(pt_ref, pe_ref, w1_ref, w2_ref, cache, o_ref, buf, xs, sem, *, layer, npg, W):
    slot = _gather_step(pt_ref, [cache], [buf], [sem], [True], layer, npg)

    group = 2 * CMP_BLOCK

    def untranspose(p, carry):
        for sg in range(KV_LANES // LANES):
            t = buf[slot, p, sg * LANES:(sg + 1) * LANES, :].T
            for gi in range(PAGE // group):
                r0 = pl.multiple_of((p * (PAGE // group) + gi) * XS_PITCH, 8)
                xs[sg, pl.ds(r0, group), :] = t[gi * group:(gi + 1) * group]
        return carry

    lax.fori_loop(0, npg, untranspose, 0)

    def load(r, n):
        rows = pl.ds(r, n, stride=XS_PITCH)
        return jnp.concatenate([xs[sg, rows, :] for sg in range(KV_LANES // LANES)], axis=1)

    _compress_compute(load, pe_ref, w1_ref, w2_ref, o_ref, npg * PAGE // CMP_BLOCK, W)


def _compress_paged(page_table, cache_t, layer, pe_row, w1bd, w2p, W):
    DB, npg = page_table.shape
    grid_spec = pltpu.PrefetchScalarGridSpec(
        num_scalar_prefetch=1,
        grid=(DB,),
        in_specs=_compress_weight_specs(lambda n: (lambda b, pt: (0,) * n)) + [pl.BlockSpec(memory_space=pl.ANY)],
        out_specs=pl.BlockSpec((1, 2 * W, KV_LANES), lambda b, pt: (b, 0, 0)),
        scratch_shapes=[pltpu.VMEM((2, npg, KV_LANES, PAGE), f32),
                        pltpu.VMEM((KV_LANES // LANES, npg * PAGE // (2 * CMP_BLOCK) * XS_PITCH, LANES), f32),
                        pltpu.SemaphoreType.DMA((2,))],
    )
    return pl.pallas_call(
        functools.partial(_compress_paged_kernel, layer=layer, npg=npg, W=W),
        out_shape=jax.ShapeDtypeStruct((DB, 2 * W, KV_LANES), f32),
        grid_spec=grid_spec,
        compiler_params=_cparams(("arbitrary",)),
        name="compress_paged",
    )(page_table.reshape(-1), pe_row, w1bd, w2p, cache_t)


def _masked_softmax(s):
    valid = s > 0.5 * NEG
    m = jnp.max(s, axis=-1, keepdims=True)
    e = jnp.where(valid, jnp.exp(s - m), 0.0)
    l = jnp.sum(e, axis=-1, keepdims=True)
    return e * jnp.where(l > 0.0, 1.0 / l, 0.0)


def _select_blocks(imp, q_pos, n_blk):
    R, W = imp.shape
    blk = lax.broadcasted_iota(jnp.int32, (R, W), 1)
    cur = q_pos // SEL_BLOCK
    forced = (blk == 0) | (blk == cur) | (blk == cur - 1)
    valid = (blk * SEL_BLOCK <= q_pos) & (blk < n_blk)
    score = jnp.where(valid, jnp.where(forced, FORCE, imp), -1.0)
    score = jnp.where(blk < n_blk, score, -jnp.inf)
    rank = jnp.zeros((R, W), f32)
    for c in range(n_blk):
        col = score[:, c:c + 1]
        beats = (col > score) | ((col == score) & (blk > c))
        rank = rank + jnp.where(beats, 1.0, 0.0)
    return jnp.where((rank < float(min(N_SEL, n_blk))) & valid, 1.0, 0.0)


def _nsa_cmp_prompt_kernel(q_ref, kv_ref, bias_ref, oc_ref, sel_ref, *, tq, W, n_blk):
    i = pl.program_id(2)
    kv = kv_ref[0]
    k = kv[:, :NSA_DH].astype(bf16)
    v = kv[:, NSA_DH:].astype(bf16)
    imp = jnp.zeros((tq, 2 * W), f32)
    for h in range(NSA_HPG):
        qh = (q_ref[0, :, h * NSA_DH:(h + 1) * NSA_DH] * NSA_SCALE).astype(bf16)
        p = _masked_softmax(_dot_nt(qh, k) + bias_ref[h])
        oc_ref[0, :, h * NSA_DH:(h + 1) * NSA_DH] = _dot(p.astype(bf16), v)
        imp = imp + p
    imp = imp[:, :W] + imp[:, W:]
    q_pos = i * tq + lax.broadcasted_iota(jnp.int32, (tq, 1), 0)
    sel_ref[0, 0] = _select_blocks(imp, q_pos, n_blk)


def _nsa_cmp_prompt(z3, kcv, bias, W, n_blk, tq=LANES):
    B, T, _ = z3.shape
    return pl.pallas_call(
        functools.partial(_nsa_cmp_prompt_kernel, tq=tq, W=W, n_blk=n_blk),
        out_shape=(jax.ShapeDtypeStruct((B, T, NSA_HEADS * NSA_DH), f32),
                   jax.ShapeDtypeStruct((B, NSA_G, T, W), f32)),
        grid=(B, NSA_G, T // tq),
        in_specs=[
            pl.BlockSpec((1, tq, NSA_HPG * NSA_DH), lambda b, g, i: (b, i, g)),
            pl.BlockSpec((1, 2 * W, 2 * NSA_DH), lambda b, g, i: (b, 0, g)),
            pl.BlockSpec((NSA_HPG, tq, 2 * W), lambda b, g, i: (g, i, 0)),
        ],
        out_specs=(pl.BlockSpec((1, tq, NSA_HPG * NSA_DH), lambda b, g, i: (b, i, g)),
                   pl.BlockSpec((1, 1, tq, W), lambda b, g, i: (b, g, i, 0))),
        compiler_params=_cparams(("parallel", "parallel", "parallel")),
        name="nsa_cmp_prompt",
    )(z3, kcv, bias)


def _nsa_cmp_sample_kernel(q_ref, kv_ref, bias_ref, e_ref, oc_ref, sel_ref, mexp_ref, imp_s, *,
                           bt, ts, W, n_blk, past):
    for bb in range(bt):
        for g in range(NSA_G):
            k = kv_ref[bb, :, g * 2 * NSA_DH:g * 2 * NSA_DH + NSA_DH].astype(bf16)
            v = kv_ref[bb, :, g * 2 * NSA_DH + NSA_DH:(g + 1) * 2 * NSA_DH].astype(bf16)
            qg = (q_ref[bb, g] * NSA_SCALE).astype(bf16)
            p = _masked_softmax(_dot_nt(qg, k) + bias_ref[g])
            oc_ref[bb, g] = _dot(p.astype(bf16), v)
            imp = jnp.sum(p.reshape(ts, NSA_HPG, 2 * W), axis=1)
            r0 = (bb * NSA_G + g) * ts
            imp_s[r0:r0 + ts, :] = imp[:, :W] + imp[:, W:]
    rows = bt * NSA_G * ts
    q_pos = past + lax.rem(lax.broadcasted_iota(jnp.int32, (rows, 1), 0), ts)
    sel = _select_blocks(imp_s[...], q_pos, n_blk)
    sel_ref[...] = sel.reshape(bt, NSA_G * ts, W)
    hit = _dot(sel.astype(bf16), e_ref[...])
    mexp_ref[...] = jnp.where(hit > 0.5, 0.0, NEG).reshape(bt, NSA_G * ts, mexp_ref.shape[2])


def _nsa_cmp_sample(q, kcv, bias, expand, W, n_blk, past, bt=8):
    DB, _, R, _ = q.shape
    ts = R // NSA_HPG
    K = expand.shape[1]
    rows = NSA_G * ts
    return pl.pallas_call(
        functools.partial(_nsa_cmp_sample_kernel, bt=bt, ts=ts, W=W, n_blk=n_blk, past=past),
        out_shape=(jax.ShapeDtypeStruct((DB, NSA_G, R, NSA_DH), f32),
                   jax.ShapeDtypeStruct((DB, rows, W), f32),
                   jax.ShapeDtypeStruct((DB, rows, K), f32)),
        grid=(DB // bt,),
        in_specs=[
            pl.BlockSpec((bt, NSA_G, R, NSA_DH), lambda b: (b, 0, 0, 0)),
            pl.BlockSpec((bt, 2 * W, KV_LANES), lambda b: (b, 0, 0)),
            pl.BlockSpec((NSA_G, R, 2 * W), lambda b: (0, 0, 0)),
            pl.BlockSpec((W, K), lambda b: (0, 0)),
        ],
        out_specs=(pl.BlockSpec((bt, NSA_G, R, NSA_DH), lambda b: (b, 0, 0, 0)),
                   pl.BlockSpec((bt, rows, W), lambda b: (b, 0, 0)),
                   pl.BlockSpec((bt, rows, K), lambda b: (b, 0, 0))),
        scratch_shapes=[pltpu.VMEM((bt * rows, W), f32)],
        compiler_params=_cparams(("parallel",)),
        name="nsa_cmp_sample",
    )(q, kcv, bias, expand)


def _stack_heads(q_ref, q_s, tq):
    q_s[:, 0:2 * NSA_DH] = jnp.zeros((q_s.shape[0], 2 * NSA_DH), bf16)
    for h in range(NSA_HPG):
        q_s[h * tq:(h + 1) * tq, 0:NSA_DH] = (q_ref[0, :, h * NSA_DH:(h + 1) * NSA_DH] * NSA_SCALE).astype(bf16)


def _unstack_heads(o, o_ref, tq):
    for h in range(NSA_HPG):
        o_ref[0, :, h * NSA_DH:(h + 1) * NSA_DH] = o[h * tq:(h + 1) * tq, NSA_DH:]


def _nsa_sel_kernel(ii_ref, jj_ref, fl_ref, q_ref, kv_ref, tbl_ref, sel_ref, et_ref, o_ref,
                    q_s, m_s, l_s, acc_s, *, tq, tk):
    p = pl.program_id(2)
    i = ii_ref[p]
    j = jj_ref[p]
    fl = fl_ref[p]

    @pl.when((fl & 1) == 1)
    def _():
        _stack_heads(q_ref, q_s, tq)
        selneg = jnp.where(sel_ref[0, 0] > 0.5, 0.0, NEG).astype(bf16)
        for h in range(NSA_HPG):
            q_s[h * tq:(h + 1) * tq, 2 * NSA_DH:] = selneg
        _flash_init(m_s, l_s, acc_s)

    kvb = kv_ref[0].astype(bf16)
    s = _dot_nt(q_s[...], jnp.concatenate([kvb, et_ref[...]], axis=1))
    s = _add_bias_tiles(s, tbl_ref, i, j * (tk // LANES), tbl_ref.shape[1] - 1)
    _flash_update(s, kvb, m_s, l_s, acc_s)

    @pl.when((fl & 2) == 2)
    def _():
        _unstack_heads(acc_s[...] / l_s[...], o_ref, tq)


def _nsa_sel(z3, kvcol, tbl, sel, expand_t, tk=FLASH_TK):
    B, T, _ = z3.shape
    tq = LANES
    tk = min(tk, T)
    W = sel.shape[-1]
    R = NSA_HPG * tq
    ii, jj, fl = _pairs(T // tq, lambda i: 0, lambda i: (i * tq + tq - 1) // tk)
    npairs = int(ii.shape[0])
    grid_spec = pltpu.PrefetchScalarGridSpec(
        num_scalar_prefetch=3,
        grid=(B, NSA_G, npairs),
        in_specs=[
            pl.BlockSpec((1, tq, NSA_HPG * NSA_DH), lambda b, g, p, ii, jj, fl: (b, ii[p], g)),
            pl.BlockSpec((1, tk, 2 * NSA_DH), lambda b, g, p, ii, jj, fl: (b, jj[p], kvcol + g)),
            pl.BlockSpec((1, tbl.shape[1], R, LANES), lambda b, g, p, ii, jj, fl: (g, 0, 0, 0)),
            pl.BlockSpec((1, 1, tq, W), lambda b, g, p, ii, jj, fl: (b, g, ii[p], 0)),
            pl.BlockSpec((tk, W), lambda b, g, p, ii, jj, fl: (jj[p], 0)),
        ],
        out_specs=pl.BlockSpec((1, tq, NSA_HPG * NSA_DH), lambda b, g, p, ii, jj, fl: (b, ii[p], g)),
        scratch_shapes=[pltpu.VMEM((R, 2 * NSA_DH + W), bf16), pltpu.VMEM((R, 1), f32), pltpu.VMEM((R, 1), f32),
                        pltpu.VMEM((R, 2 * NSA_DH), f32)],
    )
    return pl.pallas_call(
        functools.partial(_nsa_sel_kernel, tq=tq, tk=tk),
        out_shape=jax.ShapeDtypeStruct((B, T, NSA_HEADS * NSA_DH), f32),
        grid_spec=grid_spec,
        compiler_params=_cparams(("parallel", "parallel", "arbitrary")),
        name="nsa_sel",
    )(ii, jj, fl, z3, z3, tbl, sel, expand_t)


def _nsa_win_kernel(q_ref, kv_ref, tbl_ref, o_ref, q_s, *, tq, nsub):
    i = pl.program_id(2)
    _stack_heads(q_ref, q_s, tq)
    tile0 = jnp.maximum(i - (nsub - 1), 0)
    kvb = kv_ref[0, pl.ds(pl.multiple_of(tile0 * LANES, LANES), nsub * LANES), :].astype(bf16)
    s = _add_bias_tiles(_dot_nt(q_s[...], kvb), tbl_ref, i, tile0, tbl_ref.shape[1] - 1)
    m = jnp.max(s, axis=-1, keepdims=True)
    pr = jnp.exp(s - m)
    l = jnp.sum(pr, axis=-1, keepdims=True)
    _unstack_heads(_dot(pr.astype(bf16), kvb) / l, o_ref, tq)


def _nsa_win(z3, kvcol, tbl):
    B, T, _ = z3.shape
    tq = LANES
    nsub = tbl.shape[1] - 1
    assert T >= nsub * LANES
    R = NSA_HPG * tq
    return pl.pallas_call(
        functools.partial(_nsa_win_kernel, tq=tq, nsub=nsub),
        out_shape=jax.ShapeDtypeStruct((B, T, NSA_HEADS * NSA_DH), f32),
        grid=(B, NSA_G, T // tq),
        in_specs=[
            pl.BlockSpec((1, tq, NSA_HPG * NSA_DH), lambda b, g, i: (b, i, g)),
            pl.BlockSpec((1, T, 2 * NSA_DH), lambda b, g, i: (b, 0, kvcol + g)),
            pl.BlockSpec((1, nsub + 1, R, LANES), lambda b, g, i: (g, 0, 0, 0)),
        ],
        out_specs=pl.BlockSpec((1, tq, NSA_HPG * NSA_DH), lambda b, g, i: (b, i, g)),
        scratch_shapes=[pltpu.VMEM((R, 2 * NSA_DH), bf16)],
        compiler_params=_cparams(("parallel", "parallel", "parallel")),
        name="nsa_win",
    )(z3, z3, tbl)


def _bucket_np(n):
    exact = N_BUCKETS // 2
    nf = np.maximum(n, 1).astype(np.float32)
    large = exact + (np.log(nf / exact) / math.log(MAX_DISTANCE / exact) * (N_BUCKETS - exact)).astype(np.int32)
    return np.where(n < exact, n, np.minimum(large, N_BUCKETS - 1)).astype(np.int32)


def _bucket_starts():
    b = _bucket_np(np.arange(4 * MAX_DISTANCE))
    return [int(np.argmax(b >= k)) for k in range(N_BUCKETS)]


def _t5_bias(rel_bias, dist, mask):
    n = jnp.maximum(dist, 0)[None]
    tbl = rel_bias.astype(f32).T
    col = lambda k: tbl[:, k].reshape((NSA_HEADS,) + (1,) * dist.ndim)
    out = jnp.broadcast_to(col(0), (NSA_HEADS,) + dist.shape)
    for k, start in enumerate(_bucket_starts()):
        if k:
            out = jnp.where(n >= start, col(k), out)
    return jnp.where(mask[None], out, NEG)


def _cmp_block_of_lane(W, n_cmp):
    lane = np.arange(2 * W)
    half = n_cmp // 2
    blk = np.where(lane < W, 2 * lane, 2 * (lane - W) + 1)
    ok = np.where(lane < W, lane < half, (lane - W) < half)
    return np.where(ok, blk, -1)


def _rope_tables(pos, half):
    inv = ROPE_BASE ** (-jnp.arange(half, dtype=f32) / half)
    ang = pos.astype(f32)[:, None] * inv[None, :]
    return jnp.cos(ang), jnp.sin(ang)


def _block_diag(blocks):
    n, r, c = blocks.shape[-3:]
    eye = jnp.eye(n, dtype=blocks.dtype)
    out = blocks[..., :, :, None, :] * eye[:, None, :, None]
    return out.reshape(blocks.shape[:-3] + (n * r, n * c))


def _pages_t(cache):
    L, pool, rows = cache.shape[:3]
    nd = cache.ndim
    return cache.transpose((0, 1) + tuple(range(3, nd)) + (2,)).reshape(L, pool, -1, rows)


def kernel(x_prompt, x_sample, cache_mla_ckv, cache_mla_krope, cache_nsa_cmp, cache_nsa_sel, page_table, state_nsa_win, state_ret, state_ffn_conv, rel_bias, norm_mix, norm_ffn, final_norm, ev_w_in, mla_q_norm, mla_w_qb, mla_kv_norm, mla_w_uk, mla_w_uv, ev_w_out, od_w_in, cmp_pe, cmp_w1, cmp_w2, od_w_out, ffn_w_up, ffn_conv_w, ffn_conv_b, ffn_w_down):
    B, T, D = x_prompt.shape
    DB, TS, _ = x_sample.shape
    depth = norm_mix.shape[0]
    npg = page_table.shape[1]
    past = npg * PAGE
    Mp, Ms = B * T, DB * TS
    assert T % LANES == 0 and NSA_G * TS == 8 and past % (2 * CMP_BLOCK) == 0 and TS < CMP_BLOCK
    assert past % SEL_BLOCK + TS <= SEL_BLOCK

    pos_p = jnp.arange(T)
    pos_s = past + jnp.arange(TS)
    pos_all = jnp.concatenate([jnp.tile(pos_p, B), jnp.tile(pos_s, DB)])
    cos16, sin16 = _rope_tables(pos_all, MLA_ROPE // 2)
    cos64, sin64 = _rope_tables(pos_all, RET_D // 2)
    c32 = jnp.tile(jnp.concatenate([cos16, cos16], axis=1), (1, LANES // MLA_ROPE))
    s32 = jnp.tile(jnp.concatenate([-sin16, sin16], axis=1), (1, LANES // MLA_ROPE))
    c64 = jnp.concatenate([cos64, cos64], axis=1)
    s64 = jnp.concatenate([-sin64, sin64], axis=1)

    h = jnp.concatenate([x_prompt.reshape(Mp, D), x_sample.reshape(Ms, D)], axis=0)

    tq = LANES
    n_blk_p = T // SEL_BLOCK
    Wp = 128
    assert n_blk_p <= Wp
    lane_blk = _cmp_block_of_lane(Wp, T // CMP_BLOCK)
    end = jnp.asarray(lane_blk * CMP_BLOCK + (CMP_BLOCK - 1))
    dist = jnp.arange(T)[:, None] - end[None, :]
    bias_cmp_p = _t5_bias(rel_bias, dist, (dist >= 0) & jnp.asarray(lane_blk >= 0)[None, :])

    def flash_table(nd, window):
        r = jnp.arange(tq)[None, :, None]
        c = jnp.arange(LANES)[None, None, :]
        dd = jnp.arange(nd + 1)[:, None, None] * LANES + r - c
        mask = (dd >= 0) & (jnp.arange(nd + 1)[:, None, None] < nd)
        if window:
            mask &= dd < WINDOW
        t = _t5_bias(rel_bias, dd, mask)
        t = t.reshape(NSA_G, NSA_HPG, nd + 1, tq, LANES).transpose(0, 2, 1, 3, 4)
        return t.reshape(NSA_G, nd + 1, NSA_HPG * tq, LANES)

    tbl_sel = flash_table(3, False)
    tbl_win = flash_table(WINDOW // LANES + 1, True)
    expand_pt = (jnp.arange(T)[:, None] // SEL_BLOCK == jnp.arange(Wp)[None, :]).astype(bf16)

    n_blk_s = -(-(past + TS) // SEL_BLOCK)
    Ws = max(128, -(-n_blk_s // 128) * 128)
    n_cmp_s = past // CMP_BLOCK
    assert n_cmp_s // 2 <= Ws
    lane_blk_s = _cmp_block_of_lane(Ws, n_cmp_s)
    end_s = jnp.asarray(lane_blk_s * CMP_BLOCK + (CMP_BLOCK - 1))
    qpos_s = past + jnp.arange(TS)
    dist_s = qpos_s[:, None] - end_s[None, :]
    bias_cmp_s = _t5_bias(rel_bias, dist_s, (dist_s >= 0) & jnp.asarray(lane_blk_s >= 0)[None, :])
    bias_cmp_s = bias_cmp_s.reshape(NSA_G, NSA_HPG, TS, 2 * Ws).transpose(0, 2, 1, 3).reshape(NSA_G, TS * NSA_HPG, 2 * Ws)
    expand_s = (jnp.arange(Ws)[:, None] == jnp.arange(past)[None, :] // SEL_BLOCK).astype(bf16)

    def sample_rows_table(kpos, window):
        dd = qpos_s[:, None] - kpos[None, :]
        mask = dd >= 0
        if window:
            mask &= (dd < WINDOW) & (kpos >= 0)[None, :]
        t = _t5_bias(rel_bias, dd, mask)
        t = t.reshape(NSA_G, NSA_HPG, TS, -1).transpose(1, 0, 2, 3)
        return t.reshape(NSA_HPG * NSA_G * TS, -1)

    bias_sel_s = sample_rows_table(jnp.arange(past), False)
    wb = state_nsa_win.shape[2]
    bias_win_s = sample_rows_table(past - wb + jnp.arange(wb), True)
    new_pos = jnp.concatenate([qpos_s, jnp.full((8 - TS,), 10 ** 9, qpos_s.dtype)])
    xb_nsa = sample_rows_table(new_pos, False)
    win_pages = jnp.arange(DB, dtype=jnp.int32).reshape(DB, 1)

    tt = np.repeat(np.arange(TS), MLA_HEADS)
    xb_mla = jnp.asarray(np.where(np.arange(8)[None, :] <= tt[:, None], 0.0, NEG).astype(np.float32))[None]

    krope_t = _pages_t(cache_mla_krope)
    cmp_t = _pages_t(cache_nsa_cmp)
    sel_t = _pages_t(cache_nsa_sel)
    win_t = _pages_t(state_nsa_win)

    outs = {k: [] for k in ("ckv_p", "ckv_s", "kr_p", "kr_s", "cmp_p", "cmp_s", "sel_p", "sel_s",
                            "win_p", "win_s", "ret_p", "ret_s", "conv_p", "conv_s")}

    def split(a):
        return a[:Mp], a[Mp:]

    def pad_rows(a, n):
        return jnp.pad(a, ((0, 0), (0, n - a.shape[1]), (0, 0)))

    for layer in range(depth):
        j = layer // 2
        if layer % 2 == 0:
            w = ev_w_in[j]
            a0 = MLA_Q_RANK + MLA_KV_RANK
            c0 = a0 + MLA_ROPE
            pad = jnp.zeros((D, EV_RET - c0), f32)
            w_p = jnp.concatenate([w[:, :c0], pad, w[:, c0:]], axis=1).astype(bf16)
            z = _mm(h, w_p, g=norm_mix[layer])
            wq = mla_w_qb[j].reshape(MLA_Q_RANK, MLA_HEADS, MLA_NOPE + MLA_ROPE)
            wq = jnp.concatenate([wq[:, :, :MLA_NOPE].reshape(MLA_Q_RANK, -1), wq[:, :, MLA_NOPE:].reshape(MLA_Q_RANK, -1)],
                                 axis=1).astype(bf16)
            q = _mm(z, wq, g=mla_q_norm[j])
            post = (z, q, mla_kv_norm[j], c32, s32, c64, s64)
            ckv_p, kr_p, qp_p, rq_p, rk_p = _even_post(*post, 0, Mp)
            ckv_s, kr_s, qp_s, rq_s, rk_s = _even_post(*post, Mp, Ms)
            nr = RET_HEADS * RET_D
            nz = z.shape[1]
            z_s = z[Mp:]
            q_s = q[Mp:]
            kr_s = kr_s[:, :MLA_ROPE]
            assert (Mp + Ms) % LANES == 0
            z_chunks = z.reshape(-1, LANES, nz)
            q_chunks = q.reshape(-1, LANES, q.shape[1])

            wuk_t = mla_w_uk[j].transpose(1, 2, 0).astype(bf16)
            wuv = mla_w_uv[j].transpose(1, 0, 2).astype(bf16)

            o_mla_p = _mla_prompt(q_chunks, qp_p.reshape(B, T, -1), ckv_p.reshape(B, T, -1),
                                  kr_p.reshape(B, T, -1), wuk_t, wuv).reshape(Mp, -1)
            wuk_bd = _block_diag(wuk_t)
            qa_s = (_mm(q_s, wuk_bd) * MLA_SCALE).reshape(DB, TS * MLA_HEADS, MLA_KV_RANK)
            o_lat = _paged_attn(page_table, cache_mla_ckv, j, qa_s,
                                pad_rows(ckv_s.reshape(DB, TS, -1), 8), xb_mla, False,
                                q2=qp_s.reshape(DB, TS * MLA_HEADS, MLA_ROPE), cache2=krope_t,
                                xk2=pad_rows(kr_s.reshape(DB, TS, -1), 8))
            wuv_bd = _block_diag(wuv)
            o_mla_s = _mm(o_lat.reshape(Ms, MLA_HEADS * MLA_KV_RANK), wuv_bd)

            vcol = (EV_RET + 2 * nr) // nr
            ro_p, rs_p = _retention(rq_p.reshape(B, T, nr), rk_p.reshape(B, T, nr), z_chunks,
                                    jnp.zeros((B, RET_HEADS, RET_D, RET_D), f32), float(min(RET_D, T)),
                                    vcol=vcol, gate=z_chunks, gcol=vcol + 1, chunk_major=True)
            rv_s = z_s[:, EV_RET + 2 * nr:EV_RET + 3 * nr]
            rg_s = z_s[:, EV_RET + 3 * nr:]
            ro_s, rs_s = _retention(pad_rows(rq_s.reshape(DB, TS, nr), RET_D), pad_rows(rk_s.reshape(DB, TS, nr), RET_D),
                                    pad_rows(rv_s.reshape(DB, TS, nr), RET_D), state_ret[j], float(TS))
            ro_s = ro_s[:, :TS].reshape(Ms, RET_HEADS, RET_D)
            rn_s = ro_s * lax.rsqrt(jnp.mean(ro_s * ro_s, axis=-1, keepdims=True) + EPS)
            o_ret_s = jax.nn.silu(rg_s) * rn_s.reshape(Ms, nr)
            mix = jnp.concatenate([jnp.concatenate([o_mla_p, o_mla_s], axis=0),
                                   jnp.concatenate([ro_p.reshape(Mp, nr), o_ret_s], axis=0)], axis=1)
            h = _mm(mix, ev_w_out[j].astype(bf16), resid=h)

            outs["ckv_p"].append(ckv_p.reshape(B, T, -1))
            outs["ckv_s"].append(ckv_s.reshape(DB, TS, -1))
            outs["kr_p"].append(kr_p[:, :MLA_ROPE].reshape(B, T, -1))
            outs["kr_s"].append(kr_s.reshape(DB, TS, -1))
            outs["ret_p"].append(rs_p)
            outs["ret_s"].append(rs_s)
        else:
            w = od_w_in[j]
            nq_cols = NSA_HEADS * NSA_DH
            wkv = w[:, nq_cols:nq_cols + 3 * KV_LANES].reshape(D, 3, 2, NSA_G, NSA_DH)
            wsel = wkv[:, 1].transpose(0, 2, 1, 3).reshape(D, KV_LANES)
            wwin = wkv[:, 2].transpose(0, 2, 1, 3).reshape(D, KV_LANES)
            ngate = 3 * NSA_HEADS
            w_p = jnp.concatenate([w[:, :nq_cols + KV_LANES], wsel, wwin, w[:, nq_cols + 3 * KV_LANES:],
                                   jnp.zeros((D, 128 - ngate), f32)], axis=1).astype(bf16)
            z = _mm(h, w_p, g=norm_mix[layer])
            NZ = z.shape[1]
            c_cmp = nq_cols
            c_sel = c_cmp + KV_LANES
            c_win = c_sel + KV_LANES
            c_gate = c_win + KV_LANES
            z_p, z_s = split(z)
            z3 = z_p.reshape(B, T, NZ)
            gates = jax.nn.sigmoid(z[:, c_gate:c_gate + ngate]).reshape(-1, NSA_HEADS, 3)

            def orig_kv(a):
                return a.reshape(a.shape[0], NSA_G, 2, NSA_DH).transpose(0, 2, 1, 3)

            cmp_rows = z[:, c_cmp:c_sel].reshape(-1, 2, NSA_G, NSA_DH)
            sel_rows = orig_kv(z[:, c_sel:c_win])
            win_rows = orig_kv(z[:, c_win:c_gate])

            w1k = cmp_w1[j, 0].reshape(CMP_BLOCK, NSA_DH, -1)
            w1v = cmp_w1[j, 1].reshape(CMP_BLOCK, NSA_DH, -1)
            w1bd = _block_diag(jnp.stack([w1k, w1k, w1v, w1v], axis=1).astype(bf16))
            w2k, w2v = cmp_w2[j, 0], cmp_w2[j, 1]
            zb = jnp.zeros_like(w2k)
            w2p = jnp.concatenate([
                jnp.concatenate([w2k, zb, zb, zb], axis=1),
                jnp.concatenate([zb, zb, w2k, zb], axis=1),
                jnp.concatenate([zb, w2v, zb, zb], axis=1),
                jnp.concatenate([zb, zb, zb, w2v], axis=1)], axis=0).astype(bf16)
            pe_row = jnp.concatenate([cmp_pe[j, 0], cmp_pe[j, 0], cmp_pe[j, 1], cmp_pe[j, 1]], axis=1)

            kcv_p = _compress(z3, c_cmp // KV_LANES, pe_row, w1bd, w2p, Wp)
            oc_p, sel_p = _nsa_cmp_prompt(z3, kcv_p, bias_cmp_p, Wp, n_blk_p)
            os_p = _nsa_sel(z3, c_sel // LANES, tbl_sel, sel_p, expand_pt)
            ow_p = _nsa_win(z3, c_win // LANES, tbl_win)

            kcv_s = _compress_paged(page_table, cmp_t, j, pe_row, w1bd, w2p, Ws)
            q_s5 = z_s[:, :nq_cols].reshape(DB, TS, NSA_G, NSA_HPG, NSA_DH)
            q_tj = q_s5.transpose(0, 2, 1, 3, 4).reshape(DB, NSA_G, TS * NSA_HPG, NSA_DH)
            oc_s, sel_s, mexp_s = _nsa_cmp_sample(q_tj, kcv_s, bias_cmp_s, expand_s, Ws, n_blk_s, past)
            oc_s = oc_s.reshape(DB, NSA_G, TS, NSA_HPG, NSA_DH).transpose(0, 2, 1, 3, 4)
            q_jgt = q_s5.transpose(0, 3, 2, 1, 4) * NSA_SCALE
            lane_g = (jnp.arange(KV_LANES) // NSA_DH)[None, None, None, None, :] == jnp.arange(NSA_G)[None, None, :, None, None]
            q_aug = jnp.where(lane_g, jnp.tile(q_jgt, (1, 1, 1, 1, KV_LANES // NSA_DH)), 0.0)
            q_aug = q_aug.reshape(DB, NSA_HPG * NSA_G * TS, KV_LANES)
            new_sel = sel_rows[Mp:].reshape(DB, TS, KV_LANES)
            new_win = win_rows[Mp:].reshape(DB, TS, KV_LANES)
            cur_blk = past // SEL_BLOCK
            member = sel_s[:, :, cur_blk].reshape(DB, 1, NSA_G * TS)
            member = jnp.broadcast_to(member, (DB, NSA_HPG, NSA_G * TS)).reshape(DB, -1, 1)
            xb_sel = jnp.where(member > 0.5, xb_nsa[None], NEG)
            o_sel = _paged_attn(page_table, sel_t, j, q_aug, pad_rows(new_sel, 8), xb_sel, True,
                                bias=bias_sel_s, mask=mexp_s)
            o_win = _paged_attn(win_pages, win_t, j, q_aug, pad_rows(new_win, 8), xb_nsa[None], True,
                                bias=bias_win_s)

            def unpack(o):
                o = o.reshape(DB, NSA_HPG, NSA_G, TS, 2, NSA_G, NSA_DH)[:, :, :, :, 1]
                o = jnp.stack([o[:, :, g, :, g] for g in range(NSA_G)], axis=2)
                return o.transpose(0, 3, 2, 1, 4)

            os_s, ow_s = unpack(o_sel), unpack(o_win)

            def cat(p_, s_):
                return jnp.concatenate([p_.reshape(Mp, NSA_HEADS, NSA_DH), s_.reshape(Ms, NSA_HEADS, NSA_DH)], axis=0)

            o = (gates[..., 0:1] * cat(oc_p, oc_s) + gates[..., 1:2] * cat(os_p, os_s)
                 + gates[..., 2:3] * cat(ow_p, ow_s)).reshape(-1, nq_cols)
            h = _mm(o, od_w_out[j].astype(bf16), resid=h)

            wrows = min(WINDOW, T)
            outs["cmp_p"].append(cmp_rows[:Mp].reshape(B, T, 2, NSA_G, NSA_DH))
            outs["cmp_s"].append(cmp_rows[Mp:].reshape(DB, TS, 2, NSA_G, NSA_DH))
            outs["sel_p"].append(sel_rows[:Mp].reshape(B, T, 2, NSA_G, NSA_DH))
            outs["sel_s"].append(sel_rows[Mp:].reshape(DB, TS, 2, NSA_G, NSA_DH))
            outs["win_p"].append(win_rows[:Mp].reshape(B, T, 2, NSA_G, NSA_DH)[:, T - wrows:])
            outs["win_s"].append(win_rows[Mp:].reshape(DB, TS, 2, NSA_G, NSA_DH))

        up = _mm(h, ffn_w_up[layer].astype(bf16), g=norm_ffn[layer])
        w_down = ffn_w_down[layer].astype(bf16)
        a_s = up[Mp:, :D_FF].reshape(DB, TS, D_FF)
        a_ext_s = jnp.concatenate([state_ffn_conv[layer], a_s], axis=1)
        h_p = _ffn_down_prompt(up, ffn_conv_w[layer], ffn_conv_b[layer], w_down, h, Mp, T)
        h_s = _ffn_down_rows(up, a_ext_s[:, 1:1 + TS].reshape(Ms, D_FF), a_ext_s[:, 0:TS].reshape(Ms, D_FF),
                             ffn_conv_w[layer], ffn_conv_b[layer], w_down, h, Mp)
        h = jnp.concatenate([h_p, h_s], axis=0)
        last2 = (np.arange(B)[:, None] * T + np.arange(T - 2, T)[None, :]).reshape(-1)
        outs["conv_p"].append(up[last2][:, :D_FF].reshape(B, 2, D_FF))
        outs["conv_s"].append(a_ext_s[:, TS:])

    y = _rmsnorm(h, final_norm)
    st = lambda k: jnp.stack(outs[k])
    return (y[:Mp].reshape(B, T, D), y[Mp:].reshape(DB, TS, D),
            st("ckv_p"), st("ckv_s"), st("kr_p"), st("kr_s"),
            st("cmp_p"), st("cmp_s"), st("sel_p"), st("sel_s"),
            st("win_p"), jnp.concatenate([state_nsa_win[:, :, TS:], st("win_s")], axis=2), st("ret_p"), st("ret_s"),
            st("conv_p"), st("conv_s"))
```
